```python
import math
import jax, jax.numpy as jnp
from jax import lax
import numpy as np

D_MODEL = 4096
BATCH = 2
SEQ = 8192
DEPTH = 4

N_MIXERS = 2
N_ATTN = (DEPTH + 1) // 2
N_REC = DEPTH // 2
EPS = 1e-6

DA_HEAD_DIM = 128
DA_HEADS = D_MODEL // (2 * DA_HEAD_DIM)
DA_V_DIM = 2 * DA_HEAD_DIM
Q_BLOCK = 128

REL_BUCKETS = 32
REL_MAX_DIST = 128

HG_EXPAND = 128
HG_HEADS = D_MODEL // HG_EXPAND
HG_DK = HG_EXPAND
HG_DV = D_MODEL // HG_HEADS
HG_CHUNK = 64

D_FF = 256 * ((8 * D_MODEL // 3 + 255) // 256)
CONV_W = 3

kernel_name = "hybrid_diffattn_hgrn2_convffn"


def rms_norm(x, gain):
    xf = x.astype(jnp.float32)
    inv = lax.rsqrt(jnp.mean(xf * xf, axis=-1, keepdims=True) + EPS)
    return (xf * inv).astype(x.dtype) * gain


def t5_causal_bucket(dist):
    n = jnp.maximum(dist, 0)
    max_exact = REL_BUCKETS // 2
    nf = jnp.maximum(n, 1).astype(jnp.float32)
    large = max_exact + (jnp.log(nf / max_exact) / math.log(REL_MAX_DIST / max_exact)
                         * (REL_BUCKETS - max_exact)).astype(jnp.int32)
    large = jnp.minimum(large, REL_BUCKETS - 1)
    return jnp.where(n < max_exact, n, large)


def diff_attention(h, w_in, w_out, lam_params, subln_gain, rel_by_dist, layer_idx):
    B, S, _ = h.shape
    qkv = h @ w_in
    q, k, v = jnp.split(qkv, 3, axis=-1)
    q = q.reshape(B, S, DA_HEADS, 2, DA_HEAD_DIM).transpose(3, 0, 2, 1, 4)
    k = k.reshape(B, S, DA_HEADS, 2, DA_HEAD_DIM).transpose(3, 0, 2, 1, 4)
    v = v.reshape(B, S, DA_HEADS, DA_V_DIM).transpose(0, 2, 1, 3)

    lam_init = 0.8 - 0.6 * math.exp(-0.3 * layer_idx)
    lp = lam_params.astype(jnp.float32)
    lam = jnp.exp(jnp.sum(lp[0] * lp[1])) - jnp.exp(jnp.sum(lp[2] * lp[3])) + lam_init
    scale = DA_HEAD_DIM ** -0.5

    outs = []
    for blk in range(S // Q_BLOCK):
        q0 = blk * Q_BLOCK
        end = q0 + Q_BLOCK
        qb = q[:, :, :, q0:end]
        kb = k[:, :, :, :end]
        logits = jnp.einsum('cbhqd,cbhkd->cbhqk', qb, kb).astype(jnp.float32) * scale
        dist = (q0 + jnp.arange(Q_BLOCK))[:, None] - jnp.arange(end)[None, :]
        bias = rel_by_dist[jnp.maximum(dist, 0)].transpose(2, 0, 1)
        logits = jnp.where(dist >= 0, logits + bias.astype(jnp.float32), -jnp.inf)
        p = jax.nn.softmax(logits, axis=-1)
        a = p[0] - lam * p[1]
        outs.append(jnp.einsum('bhqk,bhkv->bhqv', a.astype(v.dtype), v[:, :, :end]))
    o = jnp.concatenate(outs, axis=2)
    o = rms_norm(o, subln_gain) * (1.0 - lam_init)
    o = o.transpose(0, 2, 1, 3).reshape(B, S, DA_HEADS * DA_V_DIM)
    return o @ w_out


def hgrn2(h, w_in, w_out, g_norm_gain, lower_bound):
    B, S, _ = h.shape
    proj = h @ w_in
    q, f, i, g = jnp.split(proj, 4, axis=-1)
    q = jax.nn.silu(q.astype(jnp.float32))
    f = lower_bound + (1.0 - lower_bound) * jax.nn.sigmoid(f.astype(jnp.float32))
    k = 1.0 - f
    logf = jnp.log(f)
    nc = S // HG_CHUNK

    def to_chunks(t, d):
        return t.reshape(B, nc, HG_CHUNK, HG_HEADS, d).transpose(1, 0, 3, 2, 4)

    xs = (to_chunks(q, HG_DK), to_chunks(k, HG_DK), to_chunks(logf, HG_DK),
          to_chunks(i.astype(jnp.float32), HG_DV))
    causal = jnp.tril(jnp.ones((HG_CHUNK, HG_CHUNK), dtype=bool))[:, :, None]

    def step(state, inp):
        qc, kc, gc, vc = inp
        G = jnp.cumsum(gc, axis=-2)
        o_inter = jnp.einsum('bhtk,bhkv->bhtv', qc * jnp.exp(G), state)
        diff = G[:, :, :, None, :] - G[:, :, None, :, :]
        decay = jnp.exp(jnp.where(causal, diff, -jnp.inf))
        scores = jnp.einsum('bhtk,bhtsk,bhsk->bhts', qc, decay, kc)
        o_intra = jnp.einsum('bhts,bhsv->bhtv', scores, vc)
        G_last = G[:, :, -1:, :]
        new_state = (jnp.exp(G_last[:, :, 0, :, None]) * state
                     + jnp.einsum('bhsk,bhsv->bhkv', kc * jnp.exp(G_last - G), vc))
        return new_state, o_inter + o_intra

    init = jnp.zeros((B, HG_HEADS, HG_DK, HG_DV), jnp.float32)
    _, o = lax.scan(step, init, xs)
    o = o.transpose(1, 0, 3, 2, 4).reshape(B, S, HG_HEADS, HG_DV)
    o = rms_norm(o, g_norm_gain).reshape(B, S, HG_HEADS * HG_DV)
    o = o * jax.nn.silu(g.astype(jnp.float32))
    return o.astype(h.dtype) @ w_out


def conv_ffn(h, w_up, conv_w, conv_b, w_down):
    S = h.shape[1]
    up = h @ w_up
    gate, val = jnp.split(up, 2, axis=-1)
    gp = jnp.pad(gate, ((0, 0), (CONV_W - 1, 0), (0, 0)))
    conv = conv_b + gp[:, 0:S] * conv_w[0]
    for j in range(1, CONV_W):
        conv = conv + gp[:, j:j + S] * conv_w[j]
    return (jax.nn.silu(conv) * val) @ w_down


def setup_inputs(seed: int = 0) -> dict:
    key = jax.random.key(seed)
    ks = jax.random.split(key, 17)
    D = D_MODEL
    nrm = jax.random.normal
    return {
        "x": nrm(ks[0], (BATCH, SEQ, D), jnp.float32),
        "mix_norm": 1.0 + 0.02 * nrm(ks[1], (DEPTH, D), jnp.float32),
        "ffn_norm": 1.0 + 0.02 * nrm(ks[2], (DEPTH, D), jnp.float32),
        "final_norm": 1.0 + 0.02 * nrm(ks[3], (D,), jnp.float32),
        "rel_bias": 0.5 * nrm(ks[4], (REL_BUCKETS, DA_HEADS), jnp.float32),
        "attn_w_in": nrm(ks[5], (N_ATTN, D, 3 * D), jnp.float32) * D ** -0.5,
        "attn_lambda": 0.1 * nrm(ks[6], (N_ATTN, 4, DA_HEAD_DIM), jnp.float32),
        "attn_subln": 1.0 + 0.02 * nrm(ks[7], (N_ATTN, DA_V_DIM), jnp.float32),
        "attn_w_out": nrm(ks[8], (N_ATTN, D, D), jnp.float32) * D ** -0.5,
        "hgrn_w_in": nrm(ks[9], (N_REC, D, 4 * D), jnp.float32) * D ** -0.5,
        "hgrn_lb_logits": 0.1 * nrm(ks[10], (DEPTH, HG_HEADS * HG_DK), jnp.float32),
        "hgrn_gnorm": 1.0 + 0.02 * nrm(ks[11], (N_REC, HG_DV), jnp.float32),
        "hgrn_w_out": nrm(ks[12], (N_REC, D, D), jnp.float32) * D ** -0.5,
        "ffn_w_up": nrm(ks[13], (DEPTH, D, 2 * D_FF), jnp.float32) * D ** -0.5,
        "ffn_conv_w": nrm(ks[14], (DEPTH, CONV_W, D_FF), jnp.float32) * CONV_W ** -0.5,
        "ffn_conv_b": 0.02 * nrm(ks[15], (DEPTH, D_FF), jnp.float32),
        "ffn_w_down": nrm(ks[16], (DEPTH, D_FF, D), jnp.float32) * D_FF ** -0.5,
    }


def reference(x, mix_norm, ffn_norm, final_norm, rel_bias, attn_w_in, attn_lambda,
              attn_subln, attn_w_out, hgrn_w_in, hgrn_lb_logits, hgrn_gnorm, hgrn_w_out,
              ffn_w_up, ffn_conv_w, ffn_conv_b, ffn_w_down):
    S = x.shape[1]
    rel_by_dist = rel_bias[t5_causal_bucket(jnp.arange(S))]
    lb_p = jax.nn.softmax(hgrn_lb_logits.astype(jnp.float32), axis=0)
    lower_bounds = jnp.cumsum(lb_p, axis=0) - lb_p[0]

    h = x
    for layer in range(DEPTH):
        hn = rms_norm(h, mix_norm[layer])
        if layer % N_MIXERS == 0:
            a = layer // N_MIXERS
            h = h + diff_attention(hn, attn_w_in[a], attn_w_out[a], attn_lambda[a],
                                   attn_subln[a], rel_by_dist, layer)
        else:
            r = layer // N_MIXERS
            h = h + hgrn2(hn, hgrn_w_in[r], hgrn_w_out[r], hgrn_gnorm[r], lower_bounds[layer])
        h = h + conv_ffn(rms_norm(h, ffn_norm[layer]), ffn_w_up[layer], ffn_conv_w[layer],
                         ffn_conv_b[layer], ffn_w_down[layer])
    return rms_norm(h, final_norm)
```

```python
import functools
import math

import numpy as np
import jax
import jax.numpy as jnp
from jax import lax
from jax.experimental import pallas as pl
from jax.experimental.pallas import tpu as pltpu

EPS = 1e-6
HEAD_DIM = 128
REL_BUCKETS = 32
REL_MAX_DIST = 128
HG_CHUNK = 64
CONV_W = 3
NEG_BIG = -1e30

V7X_VMEM_BYTES = 64 * 1024 * 1024
VMEM_LIMIT = V7X_VMEM_BYTES - 8 * 1024 * 1024
SUBLANES = 8
LANES = 128

F32 = jnp.float32
BF16 = jnp.bfloat16


def _params(*sem):
    return pltpu.CompilerParams(dimension_semantics=sem, vmem_limit_bytes=VMEM_LIMIT)


def _tile(dim, pref, align):
    if dim <= pref:
        return dim
    t = (pref // align) * align
    while t >= align:
        if dim % t == 0:
            return t
        t -= align
    return dim


def _sigmoid(x):
    return 1.0 / (1.0 + jnp.exp(-x))


def _rmsnorm_kernel(x_ref, g_ref, o_ref):
    x = x_ref[...]
    inv = lax.rsqrt(jnp.mean(x * x, axis=-1, keepdims=True) + EPS)
    o_ref[...] = ((x * inv) * g_ref[...]).astype(o_ref.dtype)


def rmsnorm(x, gain, out_dtype):
    m, d = x.shape
    tm = _tile(m, 256, SUBLANES)
    return pl.pallas_call(
        _rmsnorm_kernel,
        grid=(m // tm,),
        in_specs=[pl.BlockSpec((tm, d), lambda i: (i, 0)),
                  pl.BlockSpec((1, d), lambda i: (0, 0))],
        out_specs=pl.BlockSpec((tm, d), lambda i: (i, 0)),
        out_shape=jax.ShapeDtypeStruct((m, d), out_dtype),
        compiler_params=_params("parallel"),
        name="rmsnorm",
    )(x, gain.reshape(1, d))


def _matmul_kernel(x_ref, w_ref, *rest, has_res, nk):
    if has_res:
        r_ref, o_ref, *scratch = rest
    else:
        o_ref, *scratch = rest
    part = jnp.dot(x_ref[...], w_ref[...], preferred_element_type=F32)

    def finish(acc):
        if has_res:
            acc = r_ref[...] + acc
        o_ref[...] = acc.astype(o_ref.dtype)

    if nk == 1:
        finish(part)
    else:
        acc_ref, = scratch
        k = pl.program_id(2)

        @pl.when(k == 0)
        def _():
            acc_ref[...] = part

        @pl.when(jnp.logical_and(k > 0, k < nk - 1))
        def _():
            acc_ref[...] += part

        @pl.when(k == nk - 1)
        def _():
            finish(acc_ref[...] + part)


def matmul(x, w, out_dtype, residual=None, tm=1024, tn=1024, tk=None):
    m, kdim = x.shape
    n = w.shape[1]
    tm = _tile(m, tm, SUBLANES)
    tn = _tile(n, tn, LANES)
    tk = kdim if tk is None else _tile(kdim, tk, LANES)
    nk = kdim // tk
    has_res = residual is not None
    in_specs = [pl.BlockSpec((tm, tk), lambda j, i, k: (i, k)),
                pl.BlockSpec((tk, tn), lambda j, i, k: (k, j))]
    args = [x, w]
    if has_res:
        in_specs.append(pl.BlockSpec((tm, tn), lambda j, i, k: (i, j)))
        args.append(residual)
    return pl.pallas_call(
        functools.partial(_matmul_kernel, has_res=has_res, nk=nk),
        grid=(n // tn, m // tm, nk),
        in_specs=in_specs,
        out_specs=pl.BlockSpec((tm, tn), lambda j, i, k: (i, j)),
        out_shape=jax.ShapeDtypeStruct((m, n), out_dtype),
        scratch_shapes=[pltpu.VMEM((tm, tn), F32)] if nk > 1 else [],
        compiler_params=_params("parallel", "parallel", "arbitrary"),
        name="matmul_res" if has_res else "matmul",
    )(*args)


def _ffn_up_kernel(x_ref, w_ref, cw_ref, cb_ref, o_ref, gbuf, carry, *, tiles_per_seq):
    i = pl.program_id(0)
    j = pl.program_id(1)
    tm, tn = o_ref.shape
    up = jnp.dot(x_ref[...], w_ref[...], preferred_element_type=F32)
    gate = up[:, :tn]
    val = up[:, tn:]
    seq_start = (i % tiles_per_seq) == 0

    @pl.when(seq_start)
    def _():
        gbuf[0:SUBLANES, :] = jnp.zeros((SUBLANES, tn), F32)

    @pl.when(jnp.logical_not(seq_start))
    def _():
        gbuf[0:SUBLANES, :] = carry[j]

    gbuf[SUBLANES:, :] = gate
    carry[j] = gate[tm - SUBLANES:, :]
    g1 = gbuf[SUBLANES - 1:SUBLANES - 1 + tm, :]
    g2 = gbuf[SUBLANES - 2:SUBLANES - 2 + tm, :]
    cw = cw_ref[...]
    conv = cb_ref[...] + g2 * cw[0:1, :]
    conv = conv + g1 * cw[1:2, :]
    conv = conv + gate * cw[2:3, :]
    o_ref[...] = ((conv * _sigmoid(conv)) * val).astype(o_ref.dtype)


def ffn_up(x, w_pair, conv_w, conv_b, seq_len, tm=2048, tn=256):
    m, d = x.shape
    f = conv_w.shape[1]
    tn = _tile(f, tn, LANES)
    tm = _tile(seq_len, tm, SUBLANES)
    nj = f // tn
    return pl.pallas_call(
        functools.partial(_ffn_up_kernel, tiles_per_seq=seq_len // tm),
        grid=(m // tm, nj),
        in_specs=[pl.BlockSpec((tm, d), lambda i, j: (i, 0)),
                  pl.BlockSpec((d, 2 * tn), lambda i, j: (0, j)),
                  pl.BlockSpec((CONV_W, tn), lambda i, j: (0, j)),
                  pl.BlockSpec((1, tn), lambda i, j: (0, j))],
        out_specs=pl.BlockSpec((tm, tn), lambda i, j: (i, j)),
        out_shape=jax.ShapeDtypeStruct((m, f), BF16),
        scratch_shapes=[pltpu.VMEM((tm + SUBLANES, tn), F32),
                        pltpu.VMEM((nj, SUBLANES, tn), F32)],
        compiler_params=_params("arbitrary", "arbitrary"),
        name="ffn_up",
    )(x, w_pair, conv_w, conv_b.reshape(1, f))


def interleave_gate_val(w_up, tn):
    d, f2 = w_up.shape
    f = f2 // 2
    return w_up.reshape(d, 2, f // tn, tn).transpose(0, 2, 1, 3).reshape(d, f2)


def _lane_tile(x, reps):
    return x if reps == 1 else jnp.concatenate([x] * reps, axis=1)


def _attn_kernel(lam_ref, gain_ref, bd_ref, bs_ref, q_ref, k_ref, v_ref, o_ref,
                 m_sc, l_sc, acc_sc, *, lam_init):
    t = q_ref.shape[0]
    dh = HEAD_DIM
    i = pl.program_id(2)
    scale = dh ** -0.5
    q = q_ref[...]
    qc = (q[:, :dh], q[:, dh:])

    m_sc[...] = jnp.full(m_sc.shape, NEG_BIG, F32)
    l_sc[...] = jnp.zeros(l_sc.shape, F32)
    acc_sc[...] = jnp.zeros(acc_sc.shape, F32)

    def tile_step(j, bias):
        row0 = pl.multiple_of(j * t, t)
        kt = k_ref[pl.ds(row0, t), :]
        vt = v_ref[pl.ds(row0, t), :]
        for c in range(2):
            s = lax.dot_general(qc[c], kt[:, c * dh:(c + 1) * dh],
                                (((1,), (1,)), ((), ())), preferred_element_type=F32) * scale
            if bias is not None:
                s = s + bias
            m_prev = m_sc[c]
            m_next = jnp.maximum(m_prev, jnp.max(s, axis=1, keepdims=True))
            p = jnp.exp(s - _lane_tile(m_next, t // LANES))
            alpha = jnp.exp(m_prev - m_next)
            l_sc[c] = alpha * l_sc[c] + jnp.sum(p, axis=1, keepdims=True)
            m_sc[c] = m_next
            pv = jnp.dot(p.astype(BF16), vt, preferred_element_type=F32)
            acc_sc[c] = acc_sc[c] * _lane_tile(alpha, 2 * dh // LANES) + pv

    def far_body(j, carry):
        tile_step(j, None)
        return carry

    lax.fori_loop(0, jnp.maximum(i - 1, 0), far_body, 0)

    @pl.when(i >= 1)
    def _():
        tile_step(i - 1, bs_ref[0])

    tile_step(i, bd_ref[0])

    o1 = acc_sc[0] / _lane_tile(l_sc[0], 2 * dh // LANES)
    o2 = acc_sc[1] / _lane_tile(l_sc[1], 2 * dh // LANES)
    lp = lam_ref[...]
    lam = (jnp.exp(jnp.sum(lp[0:1] * lp[1:2], axis=1, keepdims=True))
           - jnp.exp(jnp.sum(lp[2:3] * lp[3:4], axis=1, keepdims=True)) + lam_init)
    a = o1 - lam * o2
    inv = lax.rsqrt(jnp.mean(a * a, axis=-1, keepdims=True) + EPS)
    o_ref[...] = (((a * inv) * gain_ref[...]) * (1.0 - lam_init)).astype(o_ref.dtype)


def _t5_causal_bucket(dist):
    n = jnp.maximum(dist, 0)
    max_exact = REL_BUCKETS // 2
    nf = jnp.maximum(n, 1).astype(F32)
    large = max_exact + (jnp.log(nf / max_exact) / math.log(REL_MAX_DIST / max_exact)
                         * (REL_BUCKETS - max_exact)).astype(jnp.int32)
    large = jnp.minimum(large, REL_BUCKETS - 1)
    return jnp.where(n < max_exact, n, large)


def _bias_tiles(rel_bias, seq_len, t):
    assert t >= REL_MAX_DIST, "distances beyond one tile must share the last bucket"
    rel_by_dist = rel_bias[_t5_causal_bucket(jnp.arange(seq_len))]
    far = rel_bias[REL_BUCKETS - 1]
    table = (rel_by_dist[:min(2 * t, seq_len)] - far[None, :]).T
    r = np.arange(t)[:, None]
    c = np.arange(t)[None, :]
    diag = jnp.where(jnp.asarray(r >= c)[None], table[:, np.maximum(r - c, 0)], NEG_BIG)
    sub = table[:, np.minimum(t + r - c, table.shape[1] - 1)]
    return diag, sub


def diff_attention(qkv, lam_params, subln_gain, rel_bias, batch, seq_len, layer_idx, t=512):
    m, d3 = qkv.shape
    d = d3 // 3
    dv = 2 * HEAD_DIM
    heads = d // dv
    t = _tile(seq_len, t, LANES)
    nq = seq_len // t
    lam_init = 0.8 - 0.6 * math.exp(-0.3 * layer_idx)
    bias_diag, bias_sub = _bias_tiles(rel_bias, seq_len, t)
    return pl.pallas_call(
        functools.partial(_attn_kernel, lam_init=lam_init),
        grid=(batch, heads, nq),
        in_specs=[pl.BlockSpec((4, HEAD_DIM), lambda b, h, i: (0, 0)),
                  pl.BlockSpec((1, dv), lambda b, h, i: (0, 0)),
                  pl.BlockSpec((1, t, t), lambda b, h, i: (h, 0, 0)),
                  pl.BlockSpec((1, t, t), lambda b, h, i: (h, 0, 0)),
                  pl.BlockSpec((t, dv), lambda b, h, i: (b * nq + i, h)),
                  pl.BlockSpec((seq_len, dv), lambda b, h, i: (b, heads + h)),
                  pl.BlockSpec((seq_len, dv), lambda b, h, i: (b, 2 * heads + h))],
        out_specs=pl.BlockSpec((t, dv), lambda b, h, i: (b * nq + i, h)),
        out_shape=jax.ShapeDtypeStruct((m, d), BF16),
        scratch_shapes=[pltpu.VMEM((2, t, LANES), F32),
                        pltpu.VMEM((2, t, LANES), F32),
                        pltpu.VMEM((2, t, dv), F32)],
        compiler_params=_params("parallel", "parallel", "arbitrary"),
        name="diff_attention",
    )(lam_params, subln_gain.reshape(1, dv), bias_diag, bias_sub, qkv, qkv, qkv)


def _hgrn_levels():
    return [HG_CHUNK >> l for l in range(int(math.log2(HG_CHUNK)) - 1)]


def _hgrn_constants():
    c = HG_CHUNK
    t = np.arange(c)[:, None]
    s = np.arange(c)[None, :]
    tril = (s <= t).astype(np.float32)
    mats = [tril]
    masks = []
    for n in _hgrn_levels() + [2]:
        same = (t // n) == (s // n)
        masks.append(same & (t % n >= n // 2) & (s % n < n // 2))
        if n > 2:
            r = (t // n) * n + n // 2 - 1
            mats.append(tril - (s <= r).astype(np.float32))
    masks.append(t == s)
    return (np.concatenate(mats, axis=0), np.stack(masks).astype(np.float32))


def _hgrn_kernel(cum_ref, mask_ref, lbl_ref, gn_ref, zq_ref, zf_ref, vi_ref, zg_ref, o_ref,
                 state_sc, *, layer, heads_per_block):
    c = HG_CHUNK
    dk = HEAD_DIM
    n_lvl = len(_hgrn_levels())

    @pl.when(pl.program_id(2) == 0)
    def _():
        state_sc[...] = jnp.zeros(state_sc.shape, F32)

    lg = lbl_ref[...]
    e = jnp.exp(lg - jnp.max(lg, axis=0, keepdims=True))
    p = e / jnp.sum(e, axis=0, keepdims=True)
    lb_all = jnp.sum(p[1:layer + 1], axis=0, keepdims=True)

    cum = cum_ref[...]
    gain = gn_ref[...]
    odd_row = (lax.broadcasted_iota(jnp.int32, (c, dk), 0) % 2) == 1
    n_chunks = zq_ref.shape[0] // c

    def chunk_body(ci, carry):
        rows = pl.ds(pl.multiple_of(ci * c, c), c)
        for hh in range(heads_per_block):
            cols = slice(hh * dk, (hh + 1) * dk)
            lb = lb_all[:, cols]
            zq = zq_ref[rows, cols]
            q = zq * _sigmoid(zq)
            f = lb + (1.0 - lb) * _sigmoid(zf_ref[rows, cols])
            k = 1.0 - f
            logf = jnp.log(f)
            v = vi_ref[rows, cols].astype(BF16)

            hi = logf.astype(BF16)
            r1 = logf - hi.astype(F32)
            mid = r1.astype(BF16)
            lo = (r1 - mid.astype(F32)).astype(BF16)
            cs = jnp.dot(cum, jnp.concatenate([hi, mid, lo], axis=1),
                         preferred_element_type=F32)
            cs = (cs[:, :dk] + cs[:, dk:2 * dk]) + cs[:, 2 * dk:]
            g = cs[:c]

            scores = None
            for l in range(n_lvl + 2):
                if l < n_lvl:
                    w = jnp.exp(-jnp.abs(cs[(l + 1) * c:(l + 2) * c]))
                    ql, kl = q * w, k * w
                elif l == n_lvl:
                    w = jnp.where(odd_row, f, 1.0)
                    ql, kl = q * w, k
                else:
                    ql, kl = q, k
                s_l = lax.dot_general(ql.astype(BF16), kl.astype(BF16),
                                      (((1,), (1,)), ((), ())), preferred_element_type=F32)
                s_l = s_l * mask_ref[l]
                scores = s_l if scores is None else scores + s_l

            state_t = state_sc[hh]
            o = lax.dot_general((q * jnp.exp(g)).astype(BF16), state_t.astype(BF16),
                                (((1,), (1,)), ((), ())), preferred_element_type=F32)
            o = o + jnp.dot(scores.astype(BF16), v, preferred_element_type=F32)

            g_last = g[c - 1:c]
            k_dec = (k * jnp.exp(g_last - g)).astype(BF16)
            state_sc[hh] = state_t * jnp.exp(g_last) + lax.dot_general(
                v, k_dec, (((0,), (0,)), ((), ())), preferred_element_type=F32)

            inv = lax.rsqrt(jnp.mean(o * o, axis=-1, keepdims=True) + EPS)
            o = (o * inv) * gain
            zg = zg_ref[rows, cols]
            o_ref[rows, cols] = (o * (zg * _sigmoid(zg))).astype(o_ref.dtype)
        return carry

    lax.fori_loop(0, n_chunks, chunk_body, 0)


def hgrn2(proj, lb_logits, gnorm_gain, batch, seq_len, layer, heads_per_block=2, rows=512):
    m, d4 = proj.shape
    d = d4 // 4
    heads = d // HEAD_DIM
    hb = heads_per_block if heads % heads_per_block == 0 else 1
    wblk = hb * HEAD_DIM
    nh = heads // hb
    rows = _tile(seq_len, rows, HG_CHUNK)
    ns = seq_len // rows
    cum, masks = _hgrn_constants()
    depth = lb_logits.shape[0]

    def zspec(section):
        return pl.BlockSpec((rows, wblk), lambda b, h, s: (b * ns + s, section * nh + h))

    return pl.pallas_call(
        functools.partial(_hgrn_kernel, layer=layer, heads_per_block=hb),
        grid=(batch, nh, ns),
        in_specs=[pl.BlockSpec(cum.shape, lambda b, h, s: (0, 0)),
                  pl.BlockSpec(masks.shape, lambda b, h, s: (0, 0, 0)),
                  pl.BlockSpec((depth, wblk), lambda b, h, s: (0, h)),
                  pl.BlockSpec((1, HEAD_DIM), lambda b, h, s: (0, 0)),
                  zspec(0), zspec(1), zspec(2), zspec(3)],
        out_specs=pl.BlockSpec((rows, wblk), lambda b, h, s: (b * ns + s, h)),
        out_shape=jax.ShapeDtypeStruct((m, d), BF16),
        scratch_shapes=[pltpu.VMEM((hb, HEAD_DIM, HEAD_DIM), F32)],
        compiler_params=_params("parallel", "parallel", "arbitrary"),
        name="hgrn2",
    )(jnp.asarray(cum, BF16), jnp.asarray(masks), lb_logits, gnorm_gain.reshape(1, HEAD_DIM),
      proj, proj, proj, proj)


def kernel(x, mix_norm, ffn_norm, final_norm, rel_bias, attn_w_in, attn_lambda, attn_subln,
           attn_w_out, hgrn_w_in, hgrn_lb_logits, hgrn_gnorm, hgrn_w_out, ffn_w_up, ffn_conv_w,
           ffn_conv_b, ffn_w_down):
    batch, seq_len, d = x.shape
    depth = mix_norm.shape[0]
    f = ffn_conv_w.shape[-1]
    ffn_tn = _tile(f, 256, LANES)
    h = x.reshape(batch * seq_len, d)
    for layer in range(depth):
        hn = rmsnorm(h, mix_norm[layer], BF16)
        if layer % 2 == 0:
            a = layer // 2
            qkv = matmul(hn, attn_w_in[a].astype(BF16), BF16)
            o = diff_attention(qkv, attn_lambda[a], attn_subln[a], rel_bias, batch, seq_len, layer)
            h = matmul(o, attn_w_out[a].astype(BF16), F32, residual=h)
        else:
            r = layer // 2
            proj = matmul(hn, hgrn_w_in[r].astype(BF16), F32)
            o = hgrn2(proj, hgrn_lb_logits, hgrn_gnorm[r], batch, seq_len, layer)
            h = matmul(o, hgrn_w_out[r].astype(BF16), F32, residual=h)
        hn = rmsnorm(h, ffn_norm[layer], BF16)
        w_pair = interleave_gate_val(ffn_w_up[layer], ffn_tn).astype(BF16)
        act = ffn_up(hn, w_pair, ffn_conv_w[layer], ffn_conv_b[layer], seq_len, tn=ffn_tn)
        h = matmul(act, ffn_w_down[layer].astype(BF16), F32, residual=h, tm=512, tn=512)
    return rmsnorm(h, final_norm, F32).reshape(batch, seq_len, d)
```

```python
import functools
import math

import numpy as np
import jax
import jax.numpy as jnp
from jax import lax
from jax.experimental import pallas as pl
from jax.experimental.pallas import tpu as pltpu

EPS = 1e-6
HEAD_DIM = 128
REL_BUCKETS = 32
REL_MAX_DIST = 128
HG_CHUNK = 64
CONV_W = 3
NEG_BIG = -1e30
REL_BLOCK = 128
LOG2E = math.log2(math.e)

V7X_VMEM_BYTES = 64 * 1024 * 1024
VMEM_LIMIT = V7X_VMEM_BYTES - 8 * 1024 * 1024
SUBLANES = 8
LANES = 128

F32 = jnp.float32
BF16 = jnp.bfloat16


def _params(*sem):
    return pltpu.CompilerParams(dimension_semantics=sem, vmem_limit_bytes=VMEM_LIMIT)


def _tile(dim, pref, align):
    if dim <= pref:
        return dim
    t = (pref // align) * align
    while t >= align:
        if dim % t == 0:
            return t
        t -= align
    return dim


def _sigmoid(x):
    return 1.0 / (1.0 + jnp.exp(-x))


def _rmsnorm_kernel(x_ref, g_ref, o_ref):
    x = x_ref[...]
    inv = lax.rsqrt(jnp.mean(x * x, axis=-1, keepdims=True) + EPS)
    o_ref[...] = ((x * inv) * g_ref[...]).astype(o_ref.dtype)


def rmsnorm(x, gain, out_dtype):
    m, d = x.shape
    tm = _tile(m, 256, SUBLANES)
    return pl.pallas_call(
        _rmsnorm_kernel,
        grid=(m // tm,),
        in_specs=[pl.BlockSpec((tm, d), lambda i: (i, 0)),
                  pl.BlockSpec((1, d), lambda i: (0, 0))],
        out_specs=pl.BlockSpec((tm, d), lambda i: (i, 0)),
        out_shape=jax.ShapeDtypeStruct((m, d), out_dtype),
        compiler_params=_params("parallel"),
        name="rmsnorm",
    )(x, gain.reshape(1, d))


def _matmul_kernel(x_ref, w_ref, *rest):
    acc = jnp.dot(x_ref[...], w_ref[...], preferred_element_type=F32)
    if len(rest) == 2:
        r_ref, o_ref = rest
        acc = r_ref[...] + acc
    else:
        o_ref, = rest
    o_ref[...] = acc.astype(o_ref.dtype)


def matmul(x, w, layer, out_dtype, residual=None, tm=1024, tn=1024):
    m, kdim = x.shape
    n = w.shape[2]
    tm = _tile(m, tm, SUBLANES)
    tn = _tile(n, tn, LANES)
    in_specs = [pl.BlockSpec((tm, kdim), lambda j, i: (i, 0)),
                pl.BlockSpec((None, kdim, tn), lambda j, i: (layer, 0, j))]
    args = [x, w]
    if residual is not None:
        in_specs.append(pl.BlockSpec((tm, tn), lambda j, i: (i, j)))
        args.append(residual)
    return pl.pallas_call(
        _matmul_kernel,
        grid=(n // tn, m // tm),
        in_specs=in_specs,
        out_specs=pl.BlockSpec((tm, tn), lambda j, i: (i, j)),
        out_shape=jax.ShapeDtypeStruct((m, n), out_dtype),
        compiler_params=_params("parallel", "parallel"),
        name="matmul" if residual is None else "matmul_res",
    )(*args)


def _ffn_up_kernel(x_ref, wg_ref, wv_ref, cw_ref, cb_ref, o_ref, gbuf, carry, *,
                   tiles_per_seq, n_sub):
    i = pl.program_id(0)
    j = pl.program_id(1)
    tm, tn = o_ref.shape
    sub = tm // n_sub
    seq_start = (i % tiles_per_seq) == 0

    @pl.when(seq_start)
    def _():
        gbuf[0:SUBLANES, :] = jnp.zeros((SUBLANES, tn), F32)

    @pl.when(jnp.logical_not(seq_start))
    def _():
        gbuf[0:SUBLANES, :] = carry[j]

    cw = cw_ref[...]
    cb = cb_ref[...]
    for c in range(n_sub):
        r0 = c * sub
        xs = x_ref[r0:r0 + sub, :]
        gate = jnp.dot(xs, wg_ref[...], preferred_element_type=F32)
        val = jnp.dot(xs, wv_ref[...], preferred_element_type=F32)
        g0 = SUBLANES + r0
        gbuf[g0:g0 + sub, :] = gate
        g1 = gbuf[g0 - 1:g0 - 1 + sub, :]
        g2 = gbuf[g0 - 2:g0 - 2 + sub, :]
        conv = cb + g2 * cw[0:1, :]
        conv = conv + g1 * cw[1:2, :]
        conv = conv + gate * cw[2:3, :]
        o_ref[r0:r0 + sub, :] = ((conv * _sigmoid(conv)) * val).astype(o_ref.dtype)
    carry[j] = gbuf[tm:tm + SUBLANES, :]


def ffn_up(x, w_up, layer, conv_w, conv_b, seq_len, tm=2048, tn=256, n_sub=4):
    m, d = x.shape
    f = conv_w.shape[1]
    tn = _tile(f, tn, LANES)
    tm = _tile(seq_len, tm, SUBLANES)
    nj = f // tn
    return pl.pallas_call(
        functools.partial(_ffn_up_kernel, tiles_per_seq=seq_len // tm, n_sub=n_sub),
        grid=(m // tm, nj),
        in_specs=[pl.BlockSpec((tm, d), lambda i, j: (i, 0)),
                  pl.BlockSpec((None, d, tn), lambda i, j: (layer, 0, j)),
                  pl.BlockSpec((None, d, tn), lambda i, j: (layer, 0, nj + j)),
                  pl.BlockSpec((CONV_W, tn), lambda i, j: (0, j)),
                  pl.BlockSpec((1, tn), lambda i, j: (0, j))],
        out_specs=pl.BlockSpec((tm, tn), lambda i, j: (i, j)),
        out_shape=jax.ShapeDtypeStruct((m, f), BF16),
        scratch_shapes=[pltpu.VMEM((tm + SUBLANES, tn), F32),
                        pltpu.VMEM((nj, SUBLANES, tn), F32)],
        compiler_params=_params("arbitrary", "arbitrary"),
        name="ffn_up",
    )(x, w_up, w_up, conv_w, conv_b.reshape(1, f))


def _lane_tile(x, reps):
    return x if reps == 1 else jnp.concatenate([x] * reps, axis=1)


def _lane_fold(x):
    out = x[:, :LANES]
    for b in range(1, x.shape[1] // LANES):
        out = out + x[:, b * LANES:(b + 1) * LANES]
    return out


def _attn_kernel(lam_ref, gain_ref, d0_ref, d1_ref, q_ref, k_ref, v_ref, o_ref,
                 bd_sc, bs_sc, q_sc, s_sc, m_sc, l_sc, acc_sc, *, lam_init):
    t = q_ref.shape[0]
    dh = HEAD_DIM
    nb = t // REL_BLOCK
    i = pl.program_id(2)

    @pl.when(i == 0)
    def _():
        blocks = {0: d0_ref[0], 1: d1_ref[0]}
        zero = jnp.zeros((REL_BLOCK, REL_BLOCK), F32)
        neg = jnp.full((REL_BLOCK, REL_BLOCK), NEG_BIG, F32)
        for r in range(nb):
            for c in range(nb):
                blk = blocks.get(r - c, zero if r > c else neg)
                bd_sc[r * REL_BLOCK:(r + 1) * REL_BLOCK, c * REL_BLOCK:(c + 1) * REL_BLOCK] = blk
        bs_sc[...] = jnp.zeros(bs_sc.shape, F32)
        bs_sc[0:REL_BLOCK, t - REL_BLOCK:t] = blocks[1]

    q_sc[...] = (q_ref[...].astype(F32) * (dh ** -0.5 * LOG2E)).astype(BF16)

    m_sc[...] = jnp.full(m_sc.shape, NEG_BIG, F32)
    l_sc[...] = jnp.zeros(l_sc.shape, F32)
    acc_sc[...] = jnp.zeros(acc_sc.shape, F32)

    def logits(j, slot):
        kt = k_ref[pl.ds(pl.multiple_of(j * t, t), t), :]
        for c in range(2):
            s_sc[slot, c] = lax.dot_general(q_sc[:, c * dh:(c + 1) * dh], kt[:, c * dh:(c + 1) * dh],
                                            (((1,), (1,)), ((), ())), preferred_element_type=F32)

    def softmax_pv(j, slot, bias_ref):
        vt = v_ref[pl.ds(pl.multiple_of(j * t, t), t), :]
        for c in range(2):
            s = s_sc[slot, c]
            if bias_ref is not None:
                s = s + bias_ref[...]
            m_prev = m_sc[c]
            m_next = jnp.maximum(m_prev, jnp.max(s, axis=1, keepdims=True))
            p = jnp.exp2(s - _lane_tile(m_next, t // LANES))
            alpha = jnp.exp2(m_prev - m_next)
            l_sc[c] = alpha * l_sc[c] + _lane_fold(p)
            m_sc[c] = m_next
            pv = jnp.dot(p.astype(BF16), vt, preferred_element_type=F32)
            acc_sc[c] = acc_sc[c] * _lane_tile(alpha, 2 * dh // LANES) + pv

    n_far = jnp.maximum(i - 1, 0)
    odd = n_far % 2

    @pl.when(odd == 0)
    def _():
        logits(0, 0)

    @pl.when(odd == 1)
    def _():
        logits(0, 1)
        logits(1, 0)
        softmax_pv(0, 1, None)

    def pair_body(pi, carry):
        j = odd + 2 * pi
        logits(j + 1, 1)
        softmax_pv(j, 0, None)
        logits(j + 2, 0)
        softmax_pv(j + 1, 1, None)
        return carry

    lax.fori_loop(0, n_far // 2, pair_body, 0)

    @pl.when(i == 0)
    def _():
        softmax_pv(0, 0, bd_sc)

    @pl.when(i >= 1)
    def _():
        logits(i, 1)
        softmax_pv(i - 1, 0, bs_sc)
        softmax_pv(i, 1, bd_sc)

    o1 = acc_sc[0] / jnp.sum(l_sc[0], axis=1, keepdims=True)
    o2 = acc_sc[1] / jnp.sum(l_sc[1], axis=1, keepdims=True)
    lp = lam_ref[...]
    lam = (jnp.exp(jnp.sum(lp[0:1] * lp[1:2], axis=1, keepdims=True))
           - jnp.exp(jnp.sum(lp[2:3] * lp[3:4], axis=1, keepdims=True)) + lam_init)
    a = o1 - lam * o2
    inv = lax.rsqrt(jnp.mean(a * a, axis=-1, keepdims=True) + EPS)
    o_ref[...] = (((a * inv) * gain_ref[...]) * (1.0 - lam_init)).astype(o_ref.dtype)


def _t5_causal_bucket(dist):
    n = jnp.maximum(dist, 0)
    max_exact = REL_BUCKETS // 2
    nf = jnp.maximum(n, 1).astype(F32)
    large = max_exact + (jnp.log(nf / max_exact) / math.log(REL_MAX_DIST / max_exact)
                         * (REL_BUCKETS - max_exact)).astype(jnp.int32)
    large = jnp.minimum(large, REL_BUCKETS - 1)
    return jnp.where(n < max_exact, n, large)


def _bias_blocks(rel_bias):
    rel_by_dist = rel_bias[_t5_causal_bucket(jnp.arange(2 * REL_BLOCK))]
    table = ((rel_by_dist - rel_bias[REL_BUCKETS - 1][None, :]) * LOG2E).T
    r = np.arange(REL_BLOCK)[:, None]
    c = np.arange(REL_BLOCK)[None, :]
    d0 = jnp.where(jnp.asarray(r >= c)[None], table[:, np.maximum(r - c, 0)], NEG_BIG)
    d1 = table[:, REL_BLOCK + r - c]
    return d0, d1


def diff_attention(qkv, lam_params, subln_gain, rel_bias, batch, seq_len, layer_idx, t=512):
    m, d3 = qkv.shape
    d = d3 // 3
    dv = 2 * HEAD_DIM
    heads = d // dv
    t = _tile(seq_len, t, REL_BLOCK)
    nq = seq_len // t
    lam_init = 0.8 - 0.6 * math.exp(-0.3 * layer_idx)
    d0, d1 = _bias_blocks(rel_bias)
    blk = (1, REL_BLOCK, REL_BLOCK)
    return pl.pallas_call(
        functools.partial(_attn_kernel, lam_init=lam_init),
        grid=(batch, heads, nq),
        in_specs=[pl.BlockSpec((4, HEAD_DIM), lambda b, h, i: (0, 0)),
                  pl.BlockSpec((1, dv), lambda b, h, i: (0, 0)),
                  pl.BlockSpec(blk, lambda b, h, i: (h, 0, 0)),
                  pl.BlockSpec(blk, lambda b, h, i: (h, 0, 0)),
                  pl.BlockSpec((t, dv), lambda b, h, i: (b * nq + i, h)),
                  pl.BlockSpec((seq_len, dv), lambda b, h, i: (b, heads + h)),
                  pl.BlockSpec((seq_len, dv), lambda b, h, i: (b, 2 * heads + h))],
        out_specs=pl.BlockSpec((t, dv), lambda b, h, i: (b * nq + i, h)),
        out_shape=jax.ShapeDtypeStruct((m, d), BF16),
        scratch_shapes=[pltpu.VMEM((t, t), F32),
                        pltpu.VMEM((t, t), F32),
                        pltpu.VMEM((t, dv), BF16),
                        pltpu.VMEM((2, 2, t, t), F32),
                        pltpu.VMEM((2, t, LANES), F32),
                        pltpu.VMEM((2, t, LANES), F32),
                        pltpu.VMEM((2, t, dv), F32)],
        compiler_params=_params("parallel", "parallel", "arbitrary"),
        name="diff_attention",
    )(lam_params, subln_gain.reshape(1, dv), d0, d1, qkv, qkv, qkv)


def _hgrn_levels():
    return [HG_CHUNK >> l for l in range(int(math.log2(HG_CHUNK)) - 1)]


def _hgrn_constants():
    c = HG_CHUNK
    t = np.arange(c)[:, None]
    s = np.arange(c)[None, :]
    tril = (s <= t).astype(np.float32)
    masks = []
    for n in _hgrn_levels() + [2]:
        same = (t // n) == (s // n)
        masks.append(same & (t % n >= n // 2) & (s % n < n // 2))
    masks.append(t == s)
    return tril, np.stack(masks).astype(np.float32)


def _level_reference(g, n, low_half):
    def row(r):
        return jnp.broadcast_to(g[r:r + 1, :], (SUBLANES, g.shape[1]))
    groups = []
    for gi in range(g.shape[0] // SUBLANES):
        base = gi * SUBLANES
        if n >= SUBLANES:
            groups.append(row((base // n) * n + n // 2 - 1))
        else:
            groups.append(jnp.where(low_half, row(base + 1), row(base + 5)))
    return jnp.concatenate(groups, axis=0)


def _hgrn_kernel(cum_ref, mask_ref, lbl_ref, gn_ref, zq_ref, zf_ref, vi_ref, zg_ref, o_ref,
                 state_sc, *, layer, heads_per_block):
    c = HG_CHUNK
    dk = HEAD_DIM
    n_lvl = len(_hgrn_levels())

    @pl.when(pl.program_id(2) == 0)
    def _():
        state_sc[...] = jnp.zeros(state_sc.shape, F32)

    lg = lbl_ref[...]
    e = jnp.exp(lg - jnp.max(lg, axis=0, keepdims=True))
    p = e / jnp.sum(e, axis=0, keepdims=True)
    lb_all = jnp.sum(p[1:layer + 1], axis=0, keepdims=True)

    cum = cum_ref[...]
    gain = gn_ref[...]
    odd_row = (lax.broadcasted_iota(jnp.int32, (c, dk), 0) % 2) == 1
    low_half = (lax.broadcasted_iota(jnp.int32, (SUBLANES, dk), 0) % SUBLANES) < SUBLANES // 2
    levels = _hgrn_levels()
    n_chunks = zq_ref.shape[0] // c

    def chunk_body(ci, carry):
        rows = pl.ds(pl.multiple_of(ci * c, c), c)
        heads = range(heads_per_block)
        col = [slice(hh * dk, (hh + 1) * dk) for hh in heads]

        q, f, k, v, cs = [], [], [], [], []
        for hh in heads:
            lb = lb_all[:, col[hh]]
            zq = zq_ref[rows, col[hh]]
            q.append(zq * _sigmoid(zq))
            fh = lb + (1.0 - lb) * _sigmoid(zf_ref[rows, col[hh]])
            f.append(fh)
            k.append(1.0 - fh)
            v.append(vi_ref[rows, col[hh]].astype(BF16))
            logf = jnp.log(fh)
            hi = logf.astype(BF16)
            r1 = logf - hi.astype(F32)
            mid = r1.astype(BF16)
            lo = (r1 - mid.astype(F32)).astype(BF16)
            x = jnp.dot(cum, jnp.concatenate([hi, mid, lo], axis=1),
                        preferred_element_type=F32)
            cs.append((x[:, :dk] + x[:, dk:2 * dk]) + x[:, 2 * dk:])

        scores = []
        for hh in heads:
            sc = None
            for l in range(n_lvl + 2):
                if l < n_lvl:
                    w = jnp.exp(-jnp.abs(cs[hh] - _level_reference(cs[hh], levels[l], low_half)))
                    ql, kl = q[hh] * w, k[hh] * w
                elif l == n_lvl:
                    ql, kl = q[hh] * jnp.where(odd_row, f[hh], 1.0), k[hh]
                else:
                    ql, kl = q[hh], k[hh]
                s_l = lax.dot_general(ql.astype(BF16), kl.astype(BF16),
                                      (((1,), (1,)), ((), ())), preferred_element_type=F32)
                s_l = s_l * mask_ref[l]
                sc = s_l if sc is None else sc + s_l
            scores.append(sc)

        for hh in heads:
            g = cs[hh]
            state_t = state_sc[hh]
            o = lax.dot_general((q[hh] * jnp.exp(g)).astype(BF16), state_t.astype(BF16),
                                (((1,), (1,)), ((), ())), preferred_element_type=F32)
            o = o + jnp.dot(scores[hh].astype(BF16), v[hh], preferred_element_type=F32)
            g_last = g[c - 1:c]
            k_dec = (k[hh] * jnp.exp(g_last - g)).astype(BF16)
            state_sc[hh] = state_t * jnp.exp(g_last) + lax.dot_general(
                v[hh], k_dec, (((0,), (0,)), ((), ())), preferred_element_type=F32)
            inv = lax.rsqrt(jnp.mean(o * o, axis=-1, keepdims=True) + EPS)
            o = (o * inv) * gain
            zg = zg_ref[rows, col[hh]]
            o_ref[rows, col[hh]] = (o * (zg * _sigmoid(zg))).astype(o_ref.dtype)
        return carry

    lax.fori_loop(0, n_chunks, chunk_body, 0)


def hgrn2(proj, lb_logits, gnorm_gain, batch, seq_len, layer, heads_per_block=4, rows=512):
    m, d4 = proj.shape
    d = d4 // 4
    heads = d // HEAD_DIM
    hb = heads_per_block if heads % heads_per_block == 0 else 1
    wblk = hb * HEAD_DIM
    nh = heads // hb
    rows = _tile(seq_len, rows, HG_CHUNK)
    ns = seq_len // rows
    cum, masks = _hgrn_constants()
    depth = lb_logits.shape[0]

    def zspec(section):
        return pl.BlockSpec((rows, wblk), lambda b, h, s: (b * ns + s, section * nh + h))

    return pl.pallas_call(
        functools.partial(_hgrn_kernel, layer=layer, heads_per_block=hb),
        grid=(batch, nh, ns),
        in_specs=[pl.BlockSpec(cum.shape, lambda b, h, s: (0, 0)),
                  pl.BlockSpec(masks.shape, lambda b, h, s: (0, 0, 0)),
                  pl.BlockSpec((depth, wblk), lambda b, h, s: (0, h)),
                  pl.BlockSpec((1, HEAD_DIM), lambda b, h, s: (0, 0)),
                  zspec(0), zspec(1), zspec(2), zspec(3)],
        out_specs=pl.BlockSpec((rows, wblk), lambda b, h, s: (b * ns + s, h)),
        out_shape=jax.ShapeDtypeStruct((m, d), BF16),
        scratch_shapes=[pltpu.VMEM((hb, HEAD_DIM, HEAD_DIM), F32)],
        compiler_params=_params("parallel", "parallel", "arbitrary"),
        name="hgrn2",
    )(jnp.asarray(cum, BF16), jnp.asarray(masks), lb_logits, gnorm_gain.reshape(1, HEAD_DIM),
      proj, proj, proj, proj)


def kernel(x, mix_norm, ffn_norm, final_norm, rel_bias, attn_w_in, attn_lambda, attn_subln,
           attn_w_out, hgrn_w_in, hgrn_lb_logits, hgrn_gnorm, hgrn_w_out, ffn_w_up, ffn_conv_w,
           ffn_conv_b, ffn_w_down):
    batch, seq_len, d = x.shape
    depth = mix_norm.shape[0]
    attn_w_in, attn_w_out, hgrn_w_in, hgrn_w_out, ffn_w_up, ffn_w_down = (
        w.astype(BF16) for w in (attn_w_in, attn_w_out, hgrn_w_in, hgrn_w_out, ffn_w_up, ffn_w_down))
    h = x.reshape(batch * seq_len, d)
    for layer in range(depth):
        hn = rmsnorm(h, mix_norm[layer], BF16)
        if layer % 2 == 0:
            a = layer // 2
            qkv = matmul(hn, attn_w_in, a, BF16)
            o = diff_attention(qkv, attn_lambda[a], attn_subln[a], rel_bias, batch, seq_len, layer)
            h = matmul(o, attn_w_out, a, F32, residual=h)
        else:
            r = layer // 2
            proj = matmul(hn, hgrn_w_in, r, F32)
            o = hgrn2(proj, hgrn_lb_logits, hgrn_gnorm[r], batch, seq_len, layer)
            h = matmul(o, hgrn_w_out, r, F32, residual=h)
        hn = rmsnorm(h, ffn_norm[layer], BF16)
        act = ffn_up(hn, ffn_w_up, layer, ffn_conv_w[layer], ffn_conv_b[layer], seq_len)
        h = matmul(act, ffn_w_down, layer, F32, residual=h, tm=512, tn=512)
    return rmsnorm(h, final_norm, F32).reshape(batch, seq_len, d)
```

```python
import functools
import math

import numpy as np
import jax
import jax.numpy as jnp
from jax import lax
from jax.experimental import pallas as pl
from jax.experimental.pallas import tpu as pltpu

EPS = 1e-6
HEAD_DIM = 128
REL_BUCKETS = 32
REL_MAX_DIST = 128
HG_CHUNK = 64
CONV_W = 3
NEG_BIG = -1e30
REL_BLOCK = 128
LOG2E = math.log2(math.e)

V7X_VMEM_BYTES = 64 * 1024 * 1024
VMEM_LIMIT = V7X_VMEM_BYTES - 4 * 1024 * 1024
SUBLANES = 8
LANES = 128

F32 = jnp.float32
BF16 = jnp.bfloat16


def _params(*sem):
    return pltpu.CompilerParams(dimension_semantics=sem, vmem_limit_bytes=VMEM_LIMIT)


def _tile(dim, pref, align):
    if dim <= pref:
        return dim
    t = (pref // align) * align
    while t >= align:
        if dim % t == 0:
            return t
        t -= align
    return dim


def _sigmoid(x):
    return 1.0 / (1.0 + jnp.exp(-x))


def _rmsnorm_kernel(x_ref, g_ref, o_ref):
    x = x_ref[...]
    inv = lax.rsqrt(jnp.mean(x * x, axis=-1, keepdims=True) + EPS)
    o_ref[...] = ((x * inv) * g_ref[...]).astype(o_ref.dtype)


def rmsnorm(x, gain, out_dtype):
    m, d = x.shape
    tm = _tile(m, 256, SUBLANES)
    return pl.pallas_call(
        _rmsnorm_kernel,
        grid=(m // tm,),
        in_specs=[pl.BlockSpec((tm, d), lambda i: (i, 0)),
                  pl.BlockSpec((1, d), lambda i: (0, 0))],
        out_specs=pl.BlockSpec((tm, d), lambda i: (i, 0)),
        out_shape=jax.ShapeDtypeStruct((m, d), out_dtype),
        compiler_params=_params("parallel"),
        name="rmsnorm",
    )(x, gain.reshape(1, d))


def _matmul_kernel(x_ref, w_ref, *rest, has_res, cast_w):
    rest = list(rest)
    r_ref = rest.pop(0) if has_res else None
    o_ref = rest.pop(0)
    if cast_w:
        w_sc, = rest

        @pl.when(pl.program_id(1) == 0)
        def _():
            w_sc[...] = w_ref[...].astype(BF16)

        w = w_sc[...]
    else:
        w = w_ref[...]
    acc = jnp.dot(x_ref[...], w, preferred_element_type=F32)
    if has_res:
        acc = r_ref[...] + acc
    o_ref[...] = acc.astype(o_ref.dtype)


def matmul(x, w, layer, out_dtype, residual=None, tm=1024, tn=512):
    m, kdim = x.shape
    n = w.shape[2]
    tm = _tile(m, tm, SUBLANES)
    tn = _tile(n, tn, LANES)
    has_res = residual is not None
    cast_w = w.dtype != BF16
    in_specs = [pl.BlockSpec((tm, kdim), lambda j, i: (i, 0)),
                pl.BlockSpec((None, kdim, tn), lambda j, i: (layer, 0, j))]
    args = [x, w]
    if has_res:
        in_specs.append(pl.BlockSpec((tm, tn), lambda j, i: (i, j)))
        args.append(residual)
    return pl.pallas_call(
        functools.partial(_matmul_kernel, has_res=has_res, cast_w=cast_w),
        grid=(n // tn, m // tm),
        in_specs=in_specs,
        out_specs=pl.BlockSpec((tm, tn), lambda j, i: (i, j)),
        out_shape=jax.ShapeDtypeStruct((m, n), out_dtype),
        scratch_shapes=[pltpu.VMEM((kdim, tn), BF16)] if cast_w else [],
        compiler_params=_params("parallel", "arbitrary"),
        name="matmul_res" if has_res else "matmul",
    )(*args)


def _ffn_up_kernel(x_ref, wg_ref, wv_ref, cw_ref, cb_ref, o_ref, w_sc, gbuf, *,
                   tiles_per_seq, n_sub):
    i = pl.program_id(1)
    tm, tn = o_ref.shape
    sub = tm // n_sub

    @pl.when(i == 0)
    def _():
        w_sc[:, :tn] = wg_ref[...].astype(BF16)
        w_sc[:, tn:] = wv_ref[...].astype(BF16)

    @pl.when((i % tiles_per_seq) == 0)
    def _():
        gbuf[0:SUBLANES, :] = jnp.zeros((SUBLANES, tn), F32)

    @pl.when((i % tiles_per_seq) != 0)
    def _():
        gbuf[0:SUBLANES, :] = gbuf[tm:tm + SUBLANES, :]

    cw = cw_ref[...]
    cb = cb_ref[...]
    for c in range(n_sub):
        r0 = c * sub
        up = jnp.dot(x_ref[r0:r0 + sub, :], w_sc[...], preferred_element_type=F32)
        gate = up[:, :tn]
        val = up[:, tn:]
        g0 = SUBLANES + r0
        gbuf[g0:g0 + sub, :] = gate
        g1 = gbuf[g0 - 1:g0 - 1 + sub, :]
        g2 = gbuf[g0 - 2:g0 - 2 + sub, :]
        conv = cb + g2 * cw[0:1, :]
        conv = conv + g1 * cw[1:2, :]
        conv = conv + gate * cw[2:3, :]
        o_ref[r0:r0 + sub, :] = ((conv * _sigmoid(conv)) * val).astype(o_ref.dtype)


def ffn_up(x, w_up, layer, conv_w, conv_b, seq_len, tm=2048, tn=256, n_sub=4):
    m, d = x.shape
    f = conv_w.shape[1]
    tn = _tile(f, tn, LANES)
    tm = _tile(seq_len, tm, SUBLANES)
    nj = f // tn
    return pl.pallas_call(
        functools.partial(_ffn_up_kernel, tiles_per_seq=seq_len // tm, n_sub=n_sub),
        grid=(nj, m // tm),
        in_specs=[pl.BlockSpec((tm, d), lambda j, i: (i, 0)),
                  pl.BlockSpec((None, d, tn), lambda j, i: (layer, 0, j)),
                  pl.BlockSpec((None, d, tn), lambda j, i: (layer, 0, nj + j)),
                  pl.BlockSpec((CONV_W, tn), lambda j, i: (0, j)),
                  pl.BlockSpec((1, tn), lambda j, i: (0, j))],
        out_specs=pl.BlockSpec((tm, tn), lambda j, i: (i, j)),
        out_shape=jax.ShapeDtypeStruct((m, f), BF16),
        scratch_shapes=[pltpu.VMEM((d, 2 * tn), BF16),
                        pltpu.VMEM((tm + SUBLANES, tn), F32)],
        compiler_params=_params("arbitrary", "arbitrary"),
        name="ffn_up",
    )(x, w_up, w_up, conv_w, conv_b.reshape(1, f))


def _lane_tile(x, reps):
    return x if reps == 1 else jnp.concatenate([x] * reps, axis=1)


def _lane_fold(x):
    out = x[:, :LANES]
    for b in range(1, x.shape[1] // LANES):
        out = out + x[:, b * LANES:(b + 1) * LANES]
    return out


def _attn_kernel(lam_ref, gain_ref, d0_ref, d1_ref, q_ref, k_ref, v_ref, o_ref,
                 bd_sc, bs_sc, q_sc, s_sc, m_sc, l_sc, acc_sc, *, lam_init):
    t = q_ref.shape[0]
    dh = HEAD_DIM
    nb = t // REL_BLOCK
    i = pl.program_id(2)

    @pl.when(i == 0)
    def _():
        blocks = {0: d0_ref[0], 1: d1_ref[0]}
        zero = jnp.zeros((REL_BLOCK, REL_BLOCK), F32)
        neg = jnp.full((REL_BLOCK, REL_BLOCK), NEG_BIG, F32)
        for r in range(nb):
            for c in range(nb):
                blk = blocks.get(r - c, zero if r > c else neg)
                bd_sc[r * REL_BLOCK:(r + 1) * REL_BLOCK, c * REL_BLOCK:(c + 1) * REL_BLOCK] = blk
        bs_sc[...] = jnp.zeros(bs_sc.shape, F32)
        bs_sc[0:REL_BLOCK, t - REL_BLOCK:t] = blocks[1]

    q_sc[...] = (q_ref[...].astype(F32) * (dh ** -0.5 * LOG2E)).astype(BF16)

    m_sc[...] = jnp.full(m_sc.shape, NEG_BIG, F32)
    l_sc[...] = jnp.zeros(l_sc.shape, F32)
    acc_sc[...] = jnp.zeros(acc_sc.shape, F32)

    def logits(j, slot):
        kt = k_ref[pl.ds(pl.multiple_of(j * t, t), t), :]
        for c in range(2):
            s_sc[slot, c] = lax.dot_general(q_sc[:, c * dh:(c + 1) * dh], kt[:, c * dh:(c + 1) * dh],
                                            (((1,), (1,)), ((), ())), preferred_element_type=F32)

    def softmax_pv(j, slot, bias_ref):
        vt = v_ref[pl.ds(pl.multiple_of(j * t, t), t), :]
        for c in range(2):
            s = s_sc[slot, c]
            if bias_ref is not None:
                s = s + bias_ref[...]
            m_prev = m_sc[c]
            m_next = jnp.maximum(m_prev, jnp.max(s, axis=1, keepdims=True))
            p = jnp.exp2(s - _lane_tile(m_next, t // LANES))
            alpha = jnp.exp2(m_prev - m_next)
            l_sc[c] = alpha * l_sc[c] + _lane_fold(p)
            m_sc[c] = m_next
            pv = jnp.dot(p.astype(BF16), vt, preferred_element_type=F32)
            acc_sc[c] = acc_sc[c] * _lane_tile(alpha, 2 * dh // LANES) + pv

    n_far = jnp.maximum(i - 1, 0)
    odd = n_far % 2

    @pl.when(odd == 0)
    def _():
        logits(0, 0)

    @pl.when(odd == 1)
    def _():
        logits(0, 1)
        logits(1, 0)
        softmax_pv(0, 1, None)

    def pair_body(pi, carry):
        j = odd + 2 * pi
        logits(j + 1, 1)
        softmax_pv(j, 0, None)
        logits(j + 2, 0)
        softmax_pv(j + 1, 1, None)
        return carry

    lax.fori_loop(0, n_far // 2, pair_body, 0)

    @pl.when(i == 0)
    def _():
        softmax_pv(0, 0, bd_sc)

    @pl.when(i >= 1)
    def _():
        logits(i, 1)
        softmax_pv(i - 1, 0, bs_sc)
        softmax_pv(i, 1, bd_sc)

    o1 = acc_sc[0] / jnp.sum(l_sc[0], axis=1, keepdims=True)
    o2 = acc_sc[1] / jnp.sum(l_sc[1], axis=1, keepdims=True)
    lp = lam_ref[...]
    lam = (jnp.exp(jnp.sum(lp[0:1] * lp[1:2], axis=1, keepdims=True))
           - jnp.exp(jnp.sum(lp[2:3] * lp[3:4], axis=1, keepdims=True)) + lam_init)
    a = o1 - lam * o2
    inv = lax.rsqrt(jnp.mean(a * a, axis=-1, keepdims=True) + EPS)
    o_ref[...] = (((a * inv) * gain_ref[...]) * (1.0 - lam_init)).astype(o_ref.dtype)


def _t5_causal_bucket(dist):
    n = jnp.maximum(dist, 0)
    max_exact = REL_BUCKETS // 2
    nf = jnp.maximum(n, 1).astype(F32)
    large = max_exact + (jnp.log(nf / max_exact) / math.log(REL_MAX_DIST / max_exact)
                         * (REL_BUCKETS - max_exact)).astype(jnp.int32)
    large = jnp.minimum(large, REL_BUCKETS - 1)
    return jnp.where(n < max_exact, n, large)


def _bias_blocks(rel_bias):
    rel_by_dist = rel_bias[_t5_causal_bucket(jnp.arange(2 * REL_BLOCK))]
    table = ((rel_by_dist - rel_bias[REL_BUCKETS - 1][None, :]) * LOG2E).T
    r = np.arange(REL_BLOCK)[:, None]
    c = np.arange(REL_BLOCK)[None, :]
    d0 = jnp.where(jnp.asarray(r >= c)[None], table[:, np.maximum(r - c, 0)], NEG_BIG)
    d1 = table[:, REL_BLOCK + r - c]
    return d0, d1


def diff_attention(qkv, lam_params, subln_gain, rel_bias, batch, seq_len, layer_idx, t=512):
    m, d3 = qkv.shape
    d = d3 // 3
    dv = 2 * HEAD_DIM
    heads = d // dv
    t = _tile(seq_len, t, REL_BLOCK)
    nq = seq_len // t
    lam_init = 0.8 - 0.6 * math.exp(-0.3 * layer_idx)
    d0, d1 = _bias_blocks(rel_bias)
    blk = (1, REL_BLOCK, REL_BLOCK)
    return pl.pallas_call(
        functools.partial(_attn_kernel, lam_init=lam_init),
        grid=(batch, heads, nq),
        in_specs=[pl.BlockSpec((4, HEAD_DIM), lambda b, h, i: (0, 0)),
                  pl.BlockSpec((1, dv), lambda b, h, i: (0, 0)),
                  pl.BlockSpec(blk, lambda b, h, i: (h, 0, 0)),
                  pl.BlockSpec(blk, lambda b, h, i: (h, 0, 0)),
                  pl.BlockSpec((t, dv), lambda b, h, i: (b * nq + i, h)),
                  pl.BlockSpec((seq_len, dv), lambda b, h, i: (b, heads + h)),
                  pl.BlockSpec((seq_len, dv), lambda b, h, i: (b, 2 * heads + h))],
        out_specs=pl.BlockSpec((t, dv), lambda b, h, i: (b * nq + i, h)),
        out_shape=jax.ShapeDtypeStruct((m, d), BF16),
        scratch_shapes=[pltpu.VMEM((t, t), F32),
                        pltpu.VMEM((t, t), F32),
                        pltpu.VMEM((t, dv), BF16),
                        pltpu.VMEM((2, 2, t, t), F32),
                        pltpu.VMEM((2, t, LANES), F32),
                        pltpu.VMEM((2, t, LANES), F32),
                        pltpu.VMEM((2, t, dv), F32)],
        compiler_params=_params("parallel", "parallel", "arbitrary"),
        name="diff_attention",
    )(lam_params, subln_gain.reshape(1, dv), d0, d1, qkv, qkv, qkv)


def _hgrn_levels():
    return [HG_CHUNK >> l for l in range(int(math.log2(HG_CHUNK)) - 1)]


def _hgrn_constants():
    c = HG_CHUNK
    t = np.arange(c)[:, None]
    s = np.arange(c)[None, :]
    tril = (s <= t).astype(np.float32)
    masks = []
    for n in _hgrn_levels() + [2]:
        same = (t // n) == (s // n)
        masks.append(same & (t % n >= n // 2) & (s % n < n // 2))
    masks.append(t == s)
    return tril, np.stack(masks).astype(np.float32)


def _level_reference(g, n, low_half):
    def row(r):
        return jnp.broadcast_to(g[r:r + 1, :], (SUBLANES, g.shape[1]))
    groups = []
    for gi in range(g.shape[0] // SUBLANES):
        base = gi * SUBLANES
        if n >= SUBLANES:
            groups.append(row((base // n) * n + n // 2 - 1))
        else:
            groups.append(jnp.where(low_half, row(base + 1), row(base + 5)))
    return jnp.concatenate(groups, axis=0)


def _hgrn_kernel(cum_ref, mask_ref, lbl_ref, gn_ref, zq_ref, zf_ref, vi_ref, zg_ref, o_ref,
                 state_sc, *, layer, heads_per_block):
    c = HG_CHUNK
    dk = HEAD_DIM
    n_lvl = len(_hgrn_levels())

    @pl.when(pl.program_id(2) == 0)
    def _():
        state_sc[...] = jnp.zeros(state_sc.shape, F32)

    lg = lbl_ref[...]
    e = jnp.exp(lg - jnp.max(lg, axis=0, keepdims=True))
    p = e / jnp.sum(e, axis=0, keepdims=True)
    lb_all = jnp.sum(p[1:layer + 1], axis=0, keepdims=True)

    cum = cum_ref[...]
    gain = gn_ref[...]
    odd_row = (lax.broadcasted_iota(jnp.int32, (c, dk), 0) % 2) == 1
    low_half = (lax.broadcasted_iota(jnp.int32, (SUBLANES, dk), 0) % SUBLANES) < SUBLANES // 2
    levels = _hgrn_levels()
    n_chunks = zq_ref.shape[0] // c

    def chunk_body(ci, carry):
        rows = pl.ds(pl.multiple_of(ci * c, c), c)
        heads = range(heads_per_block)
        col = [slice(hh * dk, (hh + 1) * dk) for hh in heads]

        q, f, k, v, cs = [], [], [], [], []
        for hh in heads:
            lb = lb_all[:, col[hh]]
            zq = zq_ref[rows, col[hh]]
            q.append(zq * _sigmoid(zq))
            fh = lb + (1.0 - lb) * _sigmoid(zf_ref[rows, col[hh]])
            f.append(fh)
            k.append(1.0 - fh)
            v.append(vi_ref[rows, col[hh]].astype(BF16))
            logf = jnp.log(fh)
            hi = logf.astype(BF16)
            r1 = logf - hi.astype(F32)
            mid = r1.astype(BF16)
            lo = (r1 - mid.astype(F32)).astype(BF16)
            x = jnp.dot(cum, jnp.concatenate([hi, mid, lo], axis=1),
                        preferred_element_type=F32)
            cs.append((x[:, :dk] + x[:, dk:2 * dk]) + x[:, 2 * dk:])

        scores = []
        for hh in heads:
            sc = None
            for l in range(n_lvl + 2):
                if l < n_lvl:
                    w = jnp.exp(-jnp.abs(cs[hh] - _level_reference(cs[hh], levels[l], low_half)))
                    ql, kl = q[hh] * w, k[hh] * w
                elif l == n_lvl:
                    ql, kl = q[hh] * jnp.where(odd_row, f[hh], 1.0), k[hh]
                else:
                    ql, kl = q[hh], k[hh]
                s_l = lax.dot_general(ql.astype(BF16), kl.astype(BF16),
                                      (((1,), (1,)), ((), ())), preferred_element_type=F32)
                s_l = s_l * mask_ref[l]
                sc = s_l if sc is None else sc + s_l
            scores.append(sc)

        for hh in heads:
            g = cs[hh]
            state_t = state_sc[hh]
            o = lax.dot_general((q[hh] * jnp.exp(g)).astype(BF16), state_t.astype(BF16),
                                (((1,), (1,)), ((), ())), preferred_element_type=F32)
            o = o + jnp.dot(scores[hh].astype(BF16), v[hh], preferred_element_type=F32)
            g_last = g[c - 1:c]
            k_dec = (k[hh] * jnp.exp(g_last - g)).astype(BF16)
            state_sc[hh] = state_t * jnp.exp(g_last) + lax.dot_general(
                v[hh], k_dec, (((0,), (0,)), ((), ())), preferred_element_type=F32)
            inv = lax.rsqrt(jnp.mean(o * o, axis=-1, keepdims=True) + EPS)
            o = (o * inv) * gain
            zg = zg_ref[rows, col[hh]]
            o_ref[rows, col[hh]] = (o * (zg * _sigmoid(zg))).astype(o_ref.dtype)
        return carry

    lax.fori_loop(0, n_chunks, chunk_body, 0)


def hgrn2(proj, lb_logits, gnorm_gain, batch, seq_len, layer, heads_per_block=8, rows=512):
    m, d4 = proj.shape
    d = d4 // 4
    heads = d // HEAD_DIM
    hb = heads_per_block if heads % heads_per_block == 0 else 1
    wblk = hb * HEAD_DIM
    nh = heads // hb
    rows = _tile(seq_len, rows, HG_CHUNK)
    ns = seq_len // rows
    cum, masks = _hgrn_constants()
    depth = lb_logits.shape[0]

    def zspec(section):
        return pl.BlockSpec((rows, wblk), lambda b, h, s: (b * ns + s, section * nh + h))

    return pl.pallas_call(
        functools.partial(_hgrn_kernel, layer=layer, heads_per_block=hb),
        grid=(batch, nh, ns),
        in_specs=[pl.BlockSpec(cum.shape, lambda b, h, s: (0, 0)),
                  pl.BlockSpec(masks.shape, lambda b, h, s: (0, 0, 0)),
                  pl.BlockSpec((depth, wblk), lambda b, h, s: (0, h)),
                  pl.BlockSpec((1, HEAD_DIM), lambda b, h, s: (0, 0)),
                  zspec(0), zspec(1), zspec(2), zspec(3)],
        out_specs=pl.BlockSpec((rows, wblk), lambda b, h, s: (b * ns + s, h)),
        out_shape=jax.ShapeDtypeStruct((m, d), BF16),
        scratch_shapes=[pltpu.VMEM((hb, HEAD_DIM, HEAD_DIM), F32)],
        compiler_params=_params("parallel", "parallel", "arbitrary"),
        name="hgrn2",
    )(jnp.asarray(cum, BF16), jnp.asarray(masks), lb_logits, gnorm_gain.reshape(1, HEAD_DIM),
      proj, proj, proj, proj)


def kernel(x, mix_norm, ffn_norm, final_norm, rel_bias, attn_w_in, attn_lambda, attn_subln,
           attn_w_out, hgrn_w_in, hgrn_lb_logits, hgrn_gnorm, hgrn_w_out, ffn_w_up, ffn_conv_w,
           ffn_conv_b, ffn_w_down):
    batch, seq_len, d = x.shape
    depth = mix_norm.shape[0]
    ffn_w_down = ffn_w_down.astype(BF16)
    h = x.reshape(batch * seq_len, d)
    for layer in range(depth):
        hn = rmsnorm(h, mix_norm[layer], BF16)
        if layer % 2 == 0:
            a = layer // 2
            qkv = matmul(hn, attn_w_in, a, BF16)
            o = diff_attention(qkv, attn_lambda[a], attn_subln[a], rel_bias, batch, seq_len, layer)
            h = matmul(o, attn_w_out, a, F32, residual=h)
        else:
            r = layer // 2
            proj = matmul(hn, hgrn_w_in, r, F32)
            o = hgrn2(proj, hgrn_lb_logits, hgrn_gnorm[r], batch, seq_len, layer)
            h = matmul(o, hgrn_w_out, r, F32, residual=h)
        hn = rmsnorm(h, ffn_norm[layer], BF16)
        act = ffn_up(hn, ffn_w_up, layer, ffn_conv_w[layer], ffn_conv_b[layer], seq_len)
        h = matmul(act, ffn_w_down, layer, F32, residual=h, tm=512, tn=512)
    return rmsnorm(h, final_norm, F32).reshape(batch, seq_len, d)
```

```python
import functools
import math

import numpy as np
import jax
import jax.numpy as jnp
from jax import lax
from jax.experimental import pallas as pl
from jax.experimental.pallas import tpu as pltpu

EPS = 1e-6
HEAD_DIM = 128
REL_BUCKETS = 32
REL_MAX_DIST = 128
HG_CHUNK = 64
CONV_W = 3
NEG_BIG = -1e30
REL_BLOCK = 128
LOG2E = math.log2(math.e)

V7X_VMEM_BYTES = 64 * 1024 * 1024
VMEM_LIMIT = V7X_VMEM_BYTES - 4 * 1024 * 1024
SUBLANES = 8
LANES = 128

F32 = jnp.float32
BF16 = jnp.bfloat16


def _params(*sem):
    return pltpu.CompilerParams(dimension_semantics=sem, vmem_limit_bytes=VMEM_LIMIT)


def _tile(dim, pref, align):
    if dim <= pref:
        return dim
    t = (pref // align) * align
    while t >= align:
        if dim % t == 0:
            return t
        t -= align
    return dim


def _silu(x):
    hx = 0.5 * x
    return hx + hx * jnp.tanh(hx)


def _rmsnorm_kernel(x_ref, g_ref, o_ref):
    x = x_ref[...]
    inv = lax.rsqrt(jnp.mean(x * x, axis=-1, keepdims=True) + EPS)
    o_ref[...] = ((x * inv) * g_ref[...]).astype(o_ref.dtype)


def rmsnorm(x, gain, out_dtype, tm=256):
    m, d = x.shape
    tm = _tile(m, tm, SUBLANES)
    return pl.pallas_call(
        _rmsnorm_kernel,
        grid=(m // tm,),
        in_specs=[pl.BlockSpec((tm, d), lambda i: (i, 0)),
                  pl.BlockSpec((1, d), lambda i: (0, 0))],
        out_specs=pl.BlockSpec((tm, d), lambda i: (i, 0)),
        out_shape=jax.ShapeDtypeStruct((m, d), out_dtype),
        compiler_params=_params("parallel"),
        name="rmsnorm",
    )(x, gain.reshape(1, d))


def _matmul_kernel(x_ref, w_ref, *rest, has_res, cast_w):
    rest = list(rest)
    r_ref = rest.pop(0) if has_res else None
    o_ref = rest.pop(0)
    if cast_w:
        w_sc, = rest

        @pl.when(pl.program_id(1) == 0)
        def _():
            w_sc[...] = w_ref[...].astype(BF16)

        w = w_sc[...]
    else:
        w = w_ref[...]
    acc = jnp.dot(x_ref[...], w, preferred_element_type=F32)
    if has_res:
        acc = r_ref[...] + acc
    o_ref[...] = acc.astype(o_ref.dtype)


def matmul(x, w, layer, out_dtype, residual=None, tm=512, tn=1024, single_buffer_w=False):
    m, kdim = x.shape
    n = w.shape[2]
    tm = _tile(m, tm, SUBLANES)
    tn = _tile(n, tn, LANES)
    has_res = residual is not None
    cast_w = w.dtype != BF16
    w_mode = dict(pipeline_mode=pl.Buffered(1)) if single_buffer_w else {}
    in_specs = [pl.BlockSpec((tm, kdim), lambda j, i: (i, 0)),
                pl.BlockSpec((None, kdim, tn), lambda j, i: (layer, 0, j), **w_mode)]
    args = [x, w]
    if has_res:
        in_specs.append(pl.BlockSpec((tm, tn), lambda j, i: (i, j)))
        args.append(residual)
    return pl.pallas_call(
        functools.partial(_matmul_kernel, has_res=has_res, cast_w=cast_w),
        grid=(n // tn, m // tm),
        in_specs=in_specs,
        out_specs=pl.BlockSpec((tm, tn), lambda j, i: (i, j)),
        out_shape=jax.ShapeDtypeStruct((m, n), out_dtype),
        scratch_shapes=[pltpu.VMEM((kdim, tn), BF16)] if cast_w else [],
        compiler_params=_params("parallel", "arbitrary"),
        name="matmul_res" if has_res else "matmul",
    )(*args)


def _ffn_up_kernel(x_ref, wg_ref, wv_ref, cw_ref, cb_ref, o_ref, w_sc, gbuf, *,
                   tiles_per_seq, n_sub):
    i = pl.program_id(1)
    tm, tn = o_ref.shape
    sub = tm // n_sub

    @pl.when(i == 0)
    def _():
        w_sc[:, :tn] = wg_ref[...].astype(BF16)
        w_sc[:, tn:] = wv_ref[...].astype(BF16)

    @pl.when((i % tiles_per_seq) == 0)
    def _():
        gbuf[0:SUBLANES, :] = jnp.zeros((SUBLANES, tn), F32)

    @pl.when((i % tiles_per_seq) != 0)
    def _():
        gbuf[0:SUBLANES, :] = gbuf[tm:tm + SUBLANES, :]

    cw = cw_ref[...]
    cb = cb_ref[...]
    for c in range(n_sub):
        r0 = c * sub
        up = jnp.dot(x_ref[r0:r0 + sub, :], w_sc[...], preferred_element_type=F32)
        gate = up[:, :tn]
        val = up[:, tn:]
        g0 = SUBLANES + r0
        gbuf[g0:g0 + sub, :] = gate
        g1 = gbuf[g0 - 1:g0 - 1 + sub, :]
        g2 = gbuf[g0 - 2:g0 - 2 + sub, :]
        conv = cb + g2 * cw[0:1, :]
        conv = conv + g1 * cw[1:2, :]
        conv = conv + gate * cw[2:3, :]
        o_ref[r0:r0 + sub, :] = (_silu(conv) * val).astype(o_ref.dtype)


def ffn_up(x, w_up, layer, conv_w, conv_b, seq_len, tm=2048, tn=256, n_sub=4):
    m, d = x.shape
    f = conv_w.shape[1]
    tn = _tile(f, tn, LANES)
    tm = _tile(seq_len, tm, SUBLANES)
    nj = f // tn
    return pl.pallas_call(
        functools.partial(_ffn_up_kernel, tiles_per_seq=seq_len // tm, n_sub=n_sub),
        grid=(nj, m // tm),
        in_specs=[pl.BlockSpec((tm, d), lambda j, i: (i, 0)),
                  pl.BlockSpec((None, d, tn), lambda j, i: (layer, 0, j)),
                  pl.BlockSpec((None, d, tn), lambda j, i: (layer, 0, nj + j)),
                  pl.BlockSpec((CONV_W, tn), lambda j, i: (0, j)),
                  pl.BlockSpec((1, tn), lambda j, i: (0, j))],
        out_specs=pl.BlockSpec((tm, tn), lambda j, i: (i, j)),
        out_shape=jax.ShapeDtypeStruct((m, f), BF16),
        scratch_shapes=[pltpu.VMEM((d, 2 * tn), BF16),
                        pltpu.VMEM((tm + SUBLANES, tn), F32)],
        compiler_params=_params("arbitrary", "arbitrary"),
        name="ffn_up",
    )(x, w_up, w_up, conv_w, conv_b.reshape(1, f))


def _lane_tile(x, reps):
    return x if reps == 1 else jnp.concatenate([x] * reps, axis=1)


def _lane_fold(x):
    out = x[:, :LANES]
    for b in range(1, x.shape[1] // LANES):
        out = out + x[:, b * LANES:(b + 1) * LANES]
    return out


def _attn_kernel(lam_ref, gain_ref, d0_ref, d1_ref, q_ref, k_ref, v_ref, o_ref,
                 bd_sc, bs_sc, q_sc, s_sc, m_sc, l_sc, acc_sc, *, lam_init):
    t = q_ref.shape[0]
    dh = HEAD_DIM
    nb = t // REL_BLOCK
    i = pl.program_id(2)

    @pl.when(i == 0)
    def _():
        blocks = {0: d0_ref[0], 1: d1_ref[0]}
        zero = jnp.zeros((REL_BLOCK, REL_BLOCK), F32)
        neg = jnp.full((REL_BLOCK, REL_BLOCK), NEG_BIG, F32)
        for r in range(nb):
            for c in range(nb):
                blk = blocks.get(r - c, zero if r > c else neg)
                bd_sc[r * REL_BLOCK:(r + 1) * REL_BLOCK, c * REL_BLOCK:(c + 1) * REL_BLOCK] = blk
        bs_sc[...] = jnp.zeros(bs_sc.shape, F32)
        bs_sc[0:REL_BLOCK, t - REL_BLOCK:t] = blocks[1]

    q_sc[...] = (q_ref[...].astype(F32) * (dh ** -0.5 * LOG2E)).astype(BF16)

    m_sc[...] = jnp.full(m_sc.shape, NEG_BIG, F32)
    l_sc[...] = jnp.zeros(l_sc.shape, F32)
    acc_sc[...] = jnp.zeros(acc_sc.shape, F32)

    def logits(j, slot):
        kt = k_ref[pl.ds(pl.multiple_of(j * t, t), t), :]
        for c in range(2):
            s_sc[slot, c] = lax.dot_general(q_sc[:, c * dh:(c + 1) * dh], kt[:, c * dh:(c + 1) * dh],
                                            (((1,), (1,)), ((), ())), preferred_element_type=F32)

    def softmax_pv(j, slot, bias_ref):
        vt = v_ref[pl.ds(pl.multiple_of(j * t, t), t), :]
        for c in range(2):
            s = s_sc[slot, c]
            if bias_ref is not None:
                s = s + bias_ref[...]
            m_prev = m_sc[c]
            m_next = jnp.maximum(m_prev, jnp.max(s, axis=1, keepdims=True))
            p = jnp.exp2(s - _lane_tile(m_next, t // LANES))
            alpha = jnp.exp2(m_prev - m_next)
            p = p.astype(BF16)
            l_sc[c] = alpha * l_sc[c] + _lane_fold(p.astype(F32))
            m_sc[c] = m_next
            pv = jnp.dot(p, vt, preferred_element_type=F32)
            acc_sc[c] = acc_sc[c] * _lane_tile(alpha, 2 * dh // LANES) + pv

    n_far = jnp.maximum(i - 1, 0)
    odd = n_far % 2

    @pl.when(odd == 0)
    def _():
        logits(0, 0)

    @pl.when(odd == 1)
    def _():
        logits(0, 1)
        logits(1, 0)
        softmax_pv(0, 1, None)

    def pair_body(pi, carry):
        j = odd + 2 * pi
        logits(j + 1, 1)
        softmax_pv(j, 0, None)
        logits(j + 2, 0)
        softmax_pv(j + 1, 1, None)
        return carry

    n_pairs = n_far // 2

    def quad_body(qi, carry):
        pair_body(2 * qi, carry)
        pair_body(2 * qi + 1, carry)
        return carry

    lax.fori_loop(0, n_pairs // 2, quad_body, 0)

    @pl.when(n_pairs % 2 == 1)
    def _():
        pair_body(n_pairs - 1, 0)

    @pl.when(i == 0)
    def _():
        softmax_pv(0, 0, bd_sc)

    @pl.when(i >= 1)
    def _():
        logits(i, 1)
        softmax_pv(i - 1, 0, bs_sc)
        softmax_pv(i, 1, bd_sc)

    o1 = acc_sc[0] / jnp.sum(l_sc[0], axis=1, keepdims=True)
    o2 = acc_sc[1] / jnp.sum(l_sc[1], axis=1, keepdims=True)
    lp = lam_ref[...]
    lam = (jnp.exp(jnp.sum(lp[0:1] * lp[1:2], axis=1, keepdims=True))
           - jnp.exp(jnp.sum(lp[2:3] * lp[3:4], axis=1, keepdims=True)) + lam_init)
    a = o1 - lam * o2
    inv = lax.rsqrt(jnp.mean(a * a, axis=-1, keepdims=True) + EPS)
    o_ref[...] = (((a * inv) * gain_ref[...]) * (1.0 - lam_init)).astype(o_ref.dtype)


def _t5_causal_bucket(dist):
    n = jnp.maximum(dist, 0)
    max_exact = REL_BUCKETS // 2
    nf = jnp.maximum(n, 1).astype(F32)
    large = max_exact + (jnp.log(nf / max_exact) / math.log(REL_MAX_DIST / max_exact)
                         * (REL_BUCKETS - max_exact)).astype(jnp.int32)
    large = jnp.minimum(large, REL_BUCKETS - 1)
    return jnp.where(n < max_exact, n, large)


def _bias_blocks(rel_bias):
    rel_by_dist = rel_bias[_t5_causal_bucket(jnp.arange(2 * REL_BLOCK))]
    table = ((rel_by_dist - rel_bias[REL_BUCKETS - 1][None, :]) * LOG2E).T
    r = np.arange(REL_BLOCK)[:, None]
    c = np.arange(REL_BLOCK)[None, :]
    d0 = jnp.where(jnp.asarray(r >= c)[None], table[:, np.maximum(r - c, 0)], NEG_BIG)
    d1 = table[:, REL_BLOCK + r - c]
    return d0, d1


def diff_attention(qkv, lam_params, subln_gain, rel_bias, batch, seq_len, layer_idx, t=512):
    m, d3 = qkv.shape
    d = d3 // 3
    dv = 2 * HEAD_DIM
    heads = d // dv
    t = _tile(seq_len, t, REL_BLOCK)
    nq = seq_len // t
    lam_init = 0.8 - 0.6 * math.exp(-0.3 * layer_idx)
    d0, d1 = _bias_blocks(rel_bias)
    blk = (1, REL_BLOCK, REL_BLOCK)
    return pl.pallas_call(
        functools.partial(_attn_kernel, lam_init=lam_init),
        grid=(batch, heads, nq),
        in_specs=[pl.BlockSpec((4, HEAD_DIM), lambda b, h, i: (0, 0)),
                  pl.BlockSpec((1, dv), lambda b, h, i: (0, 0)),
                  pl.BlockSpec(blk, lambda b, h, i: (h, 0, 0)),
                  pl.BlockSpec(blk, lambda b, h, i: (h, 0, 0)),
                  pl.BlockSpec((t, dv), lambda b, h, i: (b * nq + i, h)),
                  pl.BlockSpec((seq_len, dv), lambda b, h, i: (b, heads + h)),
                  pl.BlockSpec((seq_len, dv), lambda b, h, i: (b, 2 * heads + h))],
        out_specs=pl.BlockSpec((t, dv), lambda b, h, i: (b * nq + i, h)),
        out_shape=jax.ShapeDtypeStruct((m, d), BF16),
        scratch_shapes=[pltpu.VMEM((t, t), F32),
                        pltpu.VMEM((t, t), F32),
                        pltpu.VMEM((t, dv), BF16),
                        pltpu.VMEM((2, 2, t, t), F32),
                        pltpu.VMEM((2, t, LANES), F32),
                        pltpu.VMEM((2, t, LANES), F32),
                        pltpu.VMEM((2, t, dv), F32)],
        compiler_params=_params("parallel", "parallel", "arbitrary"),
        name="diff_attention",
    )(lam_params, subln_gain.reshape(1, dv), d0, d1, qkv, qkv, qkv)


def _hgrn_levels():
    return [HG_CHUNK >> l for l in range(int(math.log2(HG_CHUNK)) - 1)]


def _hgrn_constants():
    c = HG_CHUNK
    t = np.arange(c)[:, None]
    s = np.arange(c)[None, :]
    tril = (s <= t).astype(np.float32)
    masks = []
    for n in _hgrn_levels() + [2]:
        same = (t // n) == (s // n)
        masks.append(same & (t % n >= n // 2) & (s % n < n // 2))
    masks.append(t == s)
    signs = [np.where(t % n >= n // 2, 1.0, -1.0) * np.ones((1, HEAD_DIM)) for n in _hgrn_levels()]
    return tril, np.stack(masks).astype(np.float32), np.stack(signs).astype(np.float32)


def _level_reference(g, n, low_half):
    def row(r):
        return jnp.broadcast_to(g[r:r + 1, :], (SUBLANES, g.shape[1]))
    groups = []
    for gi in range(g.shape[0] // SUBLANES):
        base = gi * SUBLANES
        if n >= SUBLANES:
            groups.append(row((base // n) * n + n // 2 - 1))
        else:
            groups.append(jnp.where(low_half, row(base + 1), row(base + 5)))
    return jnp.concatenate(groups, axis=0)


def _hgrn_kernel(cum_ref, mask_ref, sign_ref, lbl_ref, gn_ref, zq_ref, zf_ref, vi_ref, zg_ref, o_ref,
                 state_sc, *, layer, heads_per_block):
    c = HG_CHUNK
    dk = HEAD_DIM
    n_lvl = len(_hgrn_levels())

    @pl.when(pl.program_id(2) == 0)
    def _():
        state_sc[...] = jnp.zeros(state_sc.shape, F32)

    lg = lbl_ref[...]
    e = jnp.exp(lg - jnp.max(lg, axis=0, keepdims=True))
    p = e / jnp.sum(e, axis=0, keepdims=True)
    lb_all = jnp.sum(p[1:layer + 1], axis=0, keepdims=True)
    f_mid = 0.5 * (1.0 + lb_all)
    f_amp = 0.5 * (1.0 - lb_all)

    cum = cum_ref[...]
    gain = gn_ref[...]
    odd_row = (lax.broadcasted_iota(jnp.int32, (c, dk), 0) % 2) == 1
    low_half = (lax.broadcasted_iota(jnp.int32, (SUBLANES, dk), 0) % SUBLANES) < SUBLANES // 2
    levels = _hgrn_levels()
    n_chunks = zq_ref.shape[0] // c

    def chunk_body(ci, carry):
        rows = pl.ds(pl.multiple_of(ci * c, c), c)
        heads = range(heads_per_block)
        col = [slice(hh * dk, (hh + 1) * dk) for hh in heads]

        q, f, k, v, cs = [], [], [], [], []
        for hh in heads:
            q.append(_silu(zq_ref[rows, col[hh]]))
            fh = f_mid[:, col[hh]] + f_amp[:, col[hh]] * jnp.tanh(0.5 * zf_ref[rows, col[hh]])
            f.append(fh)
            k.append(1.0 - fh)
            v.append(vi_ref[rows, col[hh]].astype(BF16))
            logf = jnp.log2(fh)
            hi = logf.astype(BF16)
            r1 = logf - hi.astype(F32)
            mid = r1.astype(BF16)
            lo = (r1 - mid.astype(F32)).astype(BF16)
            x = jnp.dot(cum, jnp.concatenate([hi, mid, lo], axis=1),
                        preferred_element_type=F32)
            cs.append((x[:, :dk] + x[:, dk:2 * dk]) + x[:, 2 * dk:])

        scores = []
        for hh in heads:
            sc = None
            for l in range(n_lvl + 2):
                if l < n_lvl:
                    d = cs[hh] - _level_reference(cs[hh], levels[l], low_half)
                    w = jnp.exp2(d * sign_ref[l])
                    ql, kl = q[hh] * w, k[hh] * w
                elif l == n_lvl:
                    ql, kl = q[hh] * jnp.where(odd_row, f[hh], 1.0), k[hh]
                else:
                    ql, kl = q[hh], k[hh]
                s_l = lax.dot_general(ql.astype(BF16), kl.astype(BF16),
                                      (((1,), (1,)), ((), ())), preferred_element_type=F32)
                s_l = s_l * mask_ref[l]
                sc = s_l if sc is None else sc + s_l
            scores.append(sc)

        for hh in heads:
            g = cs[hh]
            state_t = state_sc[hh]
            o = lax.dot_general((q[hh] * jnp.exp2(g)).astype(BF16), state_t.astype(BF16),
                                (((1,), (1,)), ((), ())), preferred_element_type=F32)
            o = o + jnp.dot(scores[hh].astype(BF16), v[hh], preferred_element_type=F32)
            g_last = g[c - 1:c]
            k_dec = (k[hh] * jnp.exp2(g_last - g)).astype(BF16)
            state_sc[hh] = state_t * jnp.exp2(g_last) + lax.dot_general(
                v[hh], k_dec, (((0,), (0,)), ((), ())), preferred_element_type=F32)
            inv = lax.rsqrt(jnp.mean(o * o, axis=-1, keepdims=True) + EPS)
            o = (o * inv) * gain
            o_ref[rows, col[hh]] = (o * _silu(zg_ref[rows, col[hh]])).astype(o_ref.dtype)
        return carry

    lax.fori_loop(0, n_chunks, chunk_body, 0)


def hgrn2(proj, lb_logits, gnorm_gain, batch, seq_len, layer, heads_per_block=8, rows=512):
    m, d4 = proj.shape
    d = d4 // 4
    heads = d // HEAD_DIM
    hb = heads_per_block if heads % heads_per_block == 0 else 1
    wblk = hb * HEAD_DIM
    nh = heads // hb
    rows = _tile(seq_len, rows, HG_CHUNK)
    ns = seq_len // rows
    cum, masks, signs = _hgrn_constants()
    depth = lb_logits.shape[0]

    def zspec(section):
        return pl.BlockSpec((rows, wblk), lambda b, h, s: (b * ns + s, section * nh + h))

    return pl.pallas_call(
        functools.partial(_hgrn_kernel, layer=layer, heads_per_block=hb),
        grid=(batch, nh, ns),
        in_specs=[pl.BlockSpec(cum.shape, lambda b, h, s: (0, 0)),
                  pl.BlockSpec(masks.shape, lambda b, h, s: (0, 0, 0)),
                  pl.BlockSpec(signs.shape, lambda b, h, s: (0, 0, 0)),
                  pl.BlockSpec((depth, wblk), lambda b, h, s: (0, h)),
                  pl.BlockSpec((1, HEAD_DIM), lambda b, h, s: (0, 0)),
                  zspec(0), zspec(1), zspec(2), zspec(3)],
        out_specs=pl.BlockSpec((rows, wblk), lambda b, h, s: (b * ns + s, h)),
        out_shape=jax.ShapeDtypeStruct((m, d), BF16),
        scratch_shapes=[pltpu.VMEM((hb, HEAD_DIM, HEAD_DIM), F32)],
        compiler_params=_params("parallel", "parallel", "arbitrary"),
        name="hgrn2",
    )(jnp.asarray(cum, BF16), jnp.asarray(masks), jnp.asarray(signs), lb_logits, gnorm_gain.reshape(1, HEAD_DIM),
      proj, proj, proj, proj)


def kernel(x, mix_norm, ffn_norm, final_norm, rel_bias, attn_w_in, attn_lambda, attn_subln,
           attn_w_out, hgrn_w_in, hgrn_lb_logits, hgrn_gnorm, hgrn_w_out, ffn_w_up, ffn_conv_w,
           ffn_conv_b, ffn_w_down):
    batch, seq_len, d = x.shape
    depth = mix_norm.shape[0]
    ffn_w_down = ffn_w_down.astype(BF16)
    h = x.reshape(batch * seq_len, d)
    for layer in range(depth):
        hn = rmsnorm(h, mix_norm[layer], BF16, tm=512 if layer < 2 else 256)
        if layer % 2 == 0:
            a = layer // 2
            qkv = matmul(hn, attn_w_in, a, BF16)
            o = diff_attention(qkv, attn_lambda[a], attn_subln[a], rel_bias, batch, seq_len, layer)
            h = matmul(o, attn_w_out, a, F32, residual=h)
        else:
            r = layer // 2
            proj = matmul(hn, hgrn_w_in, r, F32)
            o = hgrn2(proj, hgrn_lb_logits, hgrn_gnorm[r], batch, seq_len, layer)
            h = matmul(o, hgrn_w_out, r, F32, residual=h)
        hn = rmsnorm(h, ffn_norm[layer], BF16, tm=512 if layer < 2 else 256)
        act = ffn_up(hn, ffn_w_up, layer, ffn_conv_w[layer], ffn_conv_b[layer], seq_len)
        h = matmul(act, ffn_w_down, layer, F32, residual=h,
                   **(dict(tm=512, tn=1024, single_buffer_w=True) if layer < 2 else dict(tm=512, tn=512)))
    return rmsnorm(h, final_norm, F32).reshape(batch, seq_len, d)
```

```python
import functools
import math

import numpy as np
import jax
import jax.numpy as jnp
from jax import lax
from jax.experimental import pallas as pl
from jax.experimental.pallas import tpu as pltpu

EPS = 1e-6
HEAD_DIM = 128
REL_BUCKETS = 32
REL_MAX_DIST = 128
HG_CHUNK = 64
CONV_W = 3
NEG_BIG = -1e30
REL_BLOCK = 128
LOG2E = math.log2(math.e)

V7X_VMEM_BYTES = 64 * 1024 * 1024
VMEM_LIMIT = V7X_VMEM_BYTES - 4 * 1024 * 1024
SUBLANES = 8
LANES = 128

F32 = jnp.float32
BF16 = jnp.bfloat16


def _params(*sem):
    return pltpu.CompilerParams(dimension_semantics=sem, vmem_limit_bytes=VMEM_LIMIT)


def _tile(dim, pref, align):
    if dim <= pref:
        return dim
    t = (pref // align) * align
    while t >= align:
        if dim % t == 0:
            return t
        t -= align
    return dim


def _silu(x):
    hx = 0.5 * x
    return hx + hx * jnp.tanh(hx)


def _rmsnorm_kernel(x_ref, g_ref, o_ref):
    x = x_ref[...]
    inv = lax.rsqrt(jnp.mean(x * x, axis=-1, keepdims=True) + EPS)
    o_ref[...] = ((x * inv) * g_ref[...]).astype(o_ref.dtype)


def rmsnorm(x, gain, out_dtype, tm=512):
    m, d = x.shape
    tm = _tile(m, tm, SUBLANES)
    return pl.pallas_call(
        _rmsnorm_kernel,
        grid=(m // tm,),
        in_specs=[pl.BlockSpec((tm, d), lambda i: (i, 0)),
                  pl.BlockSpec((1, d), lambda i: (0, 0))],
        out_specs=pl.BlockSpec((tm, d), lambda i: (i, 0)),
        out_shape=jax.ShapeDtypeStruct((m, d), out_dtype),
        compiler_params=_params("parallel"),
        name="rmsnorm",
    )(x, gain.reshape(1, d))


def _matmul_kernel(x_ref, w_ref, *rest, has_res, cast_w):
    rest = list(rest)
    r_ref = rest.pop(0) if has_res else None
    o_ref = rest.pop(0)
    if cast_w:
        w_sc, = rest

        @pl.when(pl.program_id(1) == 0)
        def _():
            w_sc[...] = w_ref[...].astype(BF16)

        w = w_sc[...]
    else:
        w = w_ref[...]
    acc = jnp.dot(x_ref[...], w, preferred_element_type=F32)
    if has_res:
        acc = r_ref[...] + acc
    o_ref[...] = acc.astype(o_ref.dtype)


def matmul(x, w, layer, out_dtype, residual=None, tm=512, tn=1024):
    m, kdim = x.shape
    n = w.shape[2]
    tm = _tile(m, tm, SUBLANES)
    tn = _tile(n, tn, LANES)
    has_res = residual is not None
    cast_w = w.dtype != BF16
    in_specs = [pl.BlockSpec((tm, kdim), lambda j, i: (i, 0)),
                pl.BlockSpec((None, kdim, tn), lambda j, i: (layer, 0, j))]
    args = [x, w]
    if has_res:
        in_specs.append(pl.BlockSpec((tm, tn), lambda j, i: (i, j)))
        args.append(residual)
    return pl.pallas_call(
        functools.partial(_matmul_kernel, has_res=has_res, cast_w=cast_w),
        grid=(n // tn, m // tm),
        in_specs=in_specs,
        out_specs=pl.BlockSpec((tm, tn), lambda j, i: (i, j)),
        out_shape=jax.ShapeDtypeStruct((m, n), out_dtype),
        scratch_shapes=[pltpu.VMEM((kdim, tn), BF16)] if cast_w else [],
        compiler_params=_params("parallel", "arbitrary"),
        name="matmul_res" if has_res else "matmul",
    )(*args)


def _ffn_up_kernel(x_ref, wg_ref, wv_ref, cw_ref, cb_ref, o_ref, w_sc, gbuf, *,
                   tiles_per_seq, n_sub):
    i = pl.program_id(1)
    tm, tn = o_ref.shape
    sub = tm // n_sub

    @pl.when(i == 0)
    def _():
        w_sc[:, :tn] = wg_ref[...].astype(BF16)
        w_sc[:, tn:] = wv_ref[...].astype(BF16)

    @pl.when((i % tiles_per_seq) == 0)
    def _():
        gbuf[0:SUBLANES, :] = jnp.zeros((SUBLANES, tn), F32)

    @pl.when((i % tiles_per_seq) != 0)
    def _():
        gbuf[0:SUBLANES, :] = gbuf[tm:tm + SUBLANES, :]

    cw = cw_ref[...]
    cb = cb_ref[...]
    for c in range(n_sub):
        r0 = c * sub
        up = jnp.dot(x_ref[r0:r0 + sub, :], w_sc[...], preferred_element_type=F32)
        gate = up[:, :tn]
        val = up[:, tn:]
        g0 = SUBLANES + r0
        gbuf[g0:g0 + sub, :] = gate
        g1 = gbuf[g0 - 1:g0 - 1 + sub, :]
        g2 = gbuf[g0 - 2:g0 - 2 + sub, :]
        conv = cb + g2 * cw[0:1, :]
        conv = conv + g1 * cw[1:2, :]
        conv = conv + gate * cw[2:3, :]
        o_ref[r0:r0 + sub, :] = (_silu(conv) * val).astype(o_ref.dtype)


def ffn_up(x, w_up, layer, conv_w, conv_b, seq_len, tm=2048, tn=256, n_sub=4):
    m, d = x.shape
    f = conv_w.shape[1]
    tn = _tile(f, tn, LANES)
    tm = _tile(seq_len, tm, SUBLANES)
    nj = f // tn
    return pl.pallas_call(
        functools.partial(_ffn_up_kernel, tiles_per_seq=seq_len // tm, n_sub=n_sub),
        grid=(nj, m // tm),
        in_specs=[pl.BlockSpec((tm, d), lambda j, i: (i, 0)),
                  pl.BlockSpec((None, d, tn), lambda j, i: (layer, 0, j)),
                  pl.BlockSpec((None, d, tn), lambda j, i: (layer, 0, nj + j)),
                  pl.BlockSpec((CONV_W, tn), lambda j, i: (0, j)),
                  pl.BlockSpec((1, tn), lambda j, i: (0, j))],
        out_specs=pl.BlockSpec((tm, tn), lambda j, i: (i, j)),
        out_shape=jax.ShapeDtypeStruct((m, f), BF16),
        scratch_shapes=[pltpu.VMEM((d, 2 * tn), BF16),
                        pltpu.VMEM((tm + SUBLANES, tn), F32)],
        compiler_params=_params("arbitrary", "arbitrary"),
        name="ffn_up",
    )(x, w_up, w_up, conv_w, conv_b.reshape(1, f))


def _lane_tile(x, reps):
    return x if reps == 1 else jnp.concatenate([x] * reps, axis=1)


def _lane_fold(x):
    out = x[:, :LANES]
    for b in range(1, x.shape[1] // LANES):
        out = out + x[:, b * LANES:(b + 1) * LANES]
    return out


def _attn_kernel(lam_ref, gain_ref, d0_ref, d1_ref, q_ref, k_ref, v_ref, o_ref,
                 bd_sc, bs_sc, q_sc, s_sc, m_sc, l_sc, acc_sc, *, lam_init, t):
    dh = HEAD_DIM
    nb = t // REL_BLOCK

    blocks = {0: d0_ref[0], 1: d1_ref[0]}
    zero = jnp.zeros((REL_BLOCK, REL_BLOCK), F32)
    neg = jnp.full((REL_BLOCK, REL_BLOCK), NEG_BIG, F32)
    for r in range(nb):
        for c in range(nb):
            blk = blocks.get(r - c, zero if r > c else neg)
            bd_sc[r * REL_BLOCK:(r + 1) * REL_BLOCK, c * REL_BLOCK:(c + 1) * REL_BLOCK] = blk
    bs_sc[...] = jnp.zeros(bs_sc.shape, F32)
    bs_sc[0:REL_BLOCK, t - REL_BLOCK:t] = blocks[1]

    def logits(j, slot):
        kt = k_ref[pl.ds(pl.multiple_of(j * t, t), t), :]
        for c in range(2):
            s_sc[slot, c] = lax.dot_general(q_sc[:, c * dh:(c + 1) * dh], kt[:, c * dh:(c + 1) * dh],
                                            (((1,), (1,)), ((), ())), preferred_element_type=F32)

    def softmax_pv(j, slot, bias_ref):
        vt = v_ref[pl.ds(pl.multiple_of(j * t, t), t), :]
        for c in range(2):
            s = s_sc[slot, c]
            if bias_ref is not None:
                s = s + bias_ref[...]
            m_prev = m_sc[c]
            m_next = jnp.maximum(m_prev, jnp.max(s, axis=1, keepdims=True))
            p = jnp.exp2(s - _lane_tile(m_next, t // LANES))
            alpha = jnp.exp2(m_prev - m_next)
            p = p.astype(BF16)
            l_sc[c] = alpha * l_sc[c] + _lane_fold(p.astype(F32))
            m_sc[c] = m_next
            pv = jnp.dot(p, vt, preferred_element_type=F32)
            acc_sc[c] = acc_sc[c] * _lane_tile(alpha, 2 * dh // LANES) + pv

    def q_tile(i, carry0):
        rows = pl.ds(pl.multiple_of(i * t, t), t)
        q_sc[...] = (q_ref[rows, :].astype(F32) * (dh ** -0.5 * LOG2E)).astype(BF16)
        m_sc[...] = jnp.full(m_sc.shape, NEG_BIG, F32)
        l_sc[...] = jnp.zeros(l_sc.shape, F32)
        acc_sc[...] = jnp.zeros(acc_sc.shape, F32)

        n_far = jnp.maximum(i - 1, 0)
        odd = n_far % 2

        @pl.when(odd == 0)
        def _():
            logits(0, 0)

        @pl.when(odd == 1)
        def _():
            logits(0, 1)
            logits(1, 0)
            softmax_pv(0, 1, None)

        def pair_body(pi, carry):
            j = odd + 2 * pi
            logits(j + 1, 1)
            softmax_pv(j, 0, None)
            logits(j + 2, 0)
            softmax_pv(j + 1, 1, None)
            return carry

        n_pairs = n_far // 2

        def quad_body(qi, carry):
            pair_body(2 * qi, carry)
            pair_body(2 * qi + 1, carry)
            return carry

        lax.fori_loop(0, n_pairs // 2, quad_body, 0)

        @pl.when(n_pairs % 2 == 1)
        def _():
            pair_body(n_pairs - 1, 0)

        @pl.when(i == 0)
        def _():
            softmax_pv(0, 0, bd_sc)

        @pl.when(i >= 1)
        def _():
            logits(i, 1)
            softmax_pv(i - 1, 0, bs_sc)
            softmax_pv(i, 1, bd_sc)

        o1 = acc_sc[0] / jnp.sum(l_sc[0], axis=1, keepdims=True)
        o2 = acc_sc[1] / jnp.sum(l_sc[1], axis=1, keepdims=True)
        lp = lam_ref[...]
        lam = (jnp.exp(jnp.sum(lp[0:1] * lp[1:2], axis=1, keepdims=True))
               - jnp.exp(jnp.sum(lp[2:3] * lp[3:4], axis=1, keepdims=True)) + lam_init)
        a = o1 - lam * o2
        inv = lax.rsqrt(jnp.mean(a * a, axis=-1, keepdims=True) + EPS)
        o_ref[rows, :] = (((a * inv) * gain_ref[...]) * (1.0 - lam_init)).astype(o_ref.dtype)
        return carry0

    lax.fori_loop(0, q_ref.shape[0] // t, q_tile, 0)


def _t5_causal_bucket(dist):
    n = jnp.maximum(dist, 0)
    max_exact = REL_BUCKETS // 2
    nf = jnp.maximum(n, 1).astype(F32)
    large = max_exact + (jnp.log(nf / max_exact) / math.log(REL_MAX_DIST / max_exact)
                         * (REL_BUCKETS - max_exact)).astype(jnp.int32)
    large = jnp.minimum(large, REL_BUCKETS - 1)
    return jnp.where(n < max_exact, n, large)


def _bias_blocks(rel_bias):
    rel_by_dist = rel_bias[_t5_causal_bucket(jnp.arange(2 * REL_BLOCK))]
    table = ((rel_by_dist - rel_bias[REL_BUCKETS - 1][None, :]) * LOG2E).T
    r = np.arange(REL_BLOCK)[:, None]
    c = np.arange(REL_BLOCK)[None, :]
    d0 = jnp.where(jnp.asarray(r >= c)[None], table[:, np.maximum(r - c, 0)], NEG_BIG)
    d1 = table[:, REL_BLOCK + r - c]
    return d0, d1


def diff_attention(qkv, lam_params, subln_gain, rel_bias, batch, seq_len, layer_idx, t=512):
    m, d3 = qkv.shape
    d = d3 // 3
    dv = 2 * HEAD_DIM
    heads = d // dv
    t = _tile(seq_len, t, REL_BLOCK)
    lam_init = 0.8 - 0.6 * math.exp(-0.3 * layer_idx)
    d0, d1 = _bias_blocks(rel_bias)
    blk = (1, REL_BLOCK, REL_BLOCK)
    return pl.pallas_call(
        functools.partial(_attn_kernel, lam_init=lam_init, t=t),
        grid=(batch, heads),
        in_specs=[pl.BlockSpec((4, HEAD_DIM), lambda b, h: (0, 0)),
                  pl.BlockSpec((1, dv), lambda b, h: (0, 0)),
                  pl.BlockSpec(blk, lambda b, h: (h, 0, 0)),
                  pl.BlockSpec(blk, lambda b, h: (h, 0, 0)),
                  pl.BlockSpec((seq_len, dv), lambda b, h: (b, h)),
                  pl.BlockSpec((seq_len, dv), lambda b, h: (b, heads + h)),
                  pl.BlockSpec((seq_len, dv), lambda b, h: (b, 2 * heads + h))],
        out_specs=pl.BlockSpec((seq_len, dv), lambda b, h: (b, h)),
        out_shape=jax.ShapeDtypeStruct((m, d), BF16),
        scratch_shapes=[pltpu.VMEM((t, t), F32),
                        pltpu.VMEM((t, t), F32),
                        pltpu.VMEM((t, dv), BF16),
                        pltpu.VMEM((2, 2, t, t), F32),
                        pltpu.VMEM((2, t, LANES), F32),
                        pltpu.VMEM((2, t, LANES), F32),
                        pltpu.VMEM((2, t, dv), F32)],
        compiler_params=_params("parallel", "parallel"),
        name="diff_attention",
    )(lam_params, subln_gain.reshape(1, dv), d0, d1, qkv, qkv, qkv)


def _hgrn_levels():
    return [HG_CHUNK >> l for l in range(int(math.log2(HG_CHUNK)) - 1)]


def _hgrn_constants():
    c = HG_CHUNK
    t = np.arange(c)[:, None]
    s = np.arange(c)[None, :]
    tril = (s <= t).astype(np.float32)
    masks = []
    for n in _hgrn_levels() + [2]:
        same = (t // n) == (s // n)
        masks.append(same & (t % n >= n // 2) & (s % n < n // 2))
    masks.append(t == s)
    signs = [np.where(t % n >= n // 2, 1.0, -1.0) * np.ones((1, HEAD_DIM)) for n in _hgrn_levels()]
    return tril, np.stack(masks).astype(np.float32), np.stack(signs).astype(np.float32)


def _level_reference(g, n, low_half):
    def row(r):
        return jnp.broadcast_to(g[r:r + 1, :], (SUBLANES, g.shape[1]))
    groups = []
    for gi in range(g.shape[0] // SUBLANES):
        base = gi * SUBLANES
        if n >= SUBLANES:
            groups.append(row((base // n) * n + n // 2 - 1))
        else:
            groups.append(jnp.where(low_half, row(base + 1), row(base + 5)))
    return jnp.concatenate(groups, axis=0)


def _hgrn_kernel(cum_ref, mask_ref, sign_ref, lbl_ref, gn_ref, zq_ref, zf_ref, vi_ref, zg_ref, o_ref,
                 state_sc, *, layer, heads_per_block):
    c = HG_CHUNK
    dk = HEAD_DIM
    n_lvl = len(_hgrn_levels())

    @pl.when(pl.program_id(2) == 0)
    def _():
        state_sc[...] = jnp.zeros(state_sc.shape, F32)

    lg = lbl_ref[...]
    e = jnp.exp(lg - jnp.max(lg, axis=0, keepdims=True))
    p = e / jnp.sum(e, axis=0, keepdims=True)
    lb_all = jnp.sum(p[1:layer + 1], axis=0, keepdims=True)
    f_mid = 0.5 * (1.0 + lb_all)
    f_amp = 0.5 * (1.0 - lb_all)

    cum = cum_ref[...]
    gain = gn_ref[...]
    odd_row = (lax.broadcasted_iota(jnp.int32, (c, dk), 0) % 2) == 1
    low_half = (lax.broadcasted_iota(jnp.int32, (SUBLANES, dk), 0) % SUBLANES) < SUBLANES // 2
    levels = _hgrn_levels()
    n_chunks = zq_ref.shape[0] // c

    def chunk_body(ci, carry):
        rows = pl.ds(pl.multiple_of(ci * c, c), c)
        heads = range(heads_per_block)
        col = [slice(hh * dk, (hh + 1) * dk) for hh in heads]

        q, f, k, v, cs = [], [], [], [], []
        for hh in heads:
            q.append(_silu(zq_ref[rows, col[hh]]))
            fh = f_mid[:, col[hh]] + f_amp[:, col[hh]] * jnp.tanh(0.5 * zf_ref[rows, col[hh]])
            f.append(fh)
            k.append(1.0 - fh)
            v.append(vi_ref[rows, col[hh]].astype(BF16))
            logf = jnp.log2(fh)
            hi = logf.astype(BF16)
            r1 = logf - hi.astype(F32)
            mid = r1.astype(BF16)
            lo = (r1 - mid.astype(F32)).astype(BF16)
            x = jnp.dot(cum, jnp.concatenate([hi, mid, lo], axis=1),
                        preferred_element_type=F32)
            cs.append((x[:, :dk] + x[:, dk:2 * dk]) + x[:, 2 * dk:])

        scores = []
        for hh in heads:
            sc = None
            for l in range(n_lvl + 2):
                if l < n_lvl:
                    d = cs[hh] - _level_reference(cs[hh], levels[l], low_half)
                    w = jnp.exp2(d * sign_ref[l])
                    ql, kl = q[hh] * w, k[hh] * w
                elif l == n_lvl:
                    ql, kl = q[hh] * jnp.where(odd_row, f[hh], 1.0), k[hh]
                else:
                    ql, kl = q[hh], k[hh]
                s_l = lax.dot_general(ql.astype(BF16), kl.astype(BF16),
                                      (((1,), (1,)), ((), ())), preferred_element_type=F32)
                s_l = s_l * mask_ref[l]
                sc = s_l if sc is None else sc + s_l
            scores.append(sc)

        for hh in heads:
            g = cs[hh]
            state_t = state_sc[hh]
            o = lax.dot_general((q[hh] * jnp.exp2(g)).astype(BF16), state_t.astype(BF16),
                                (((1,), (1,)), ((), ())), preferred_element_type=F32)
            o = o + jnp.dot(scores[hh].astype(BF16), v[hh], preferred_element_type=F32)
            g_last = g[c - 1:c]
            k_dec = (k[hh] * jnp.exp2(g_last - g)).astype(BF16)
            state_sc[hh] = state_t * jnp.exp2(g_last) + lax.dot_general(
                v[hh], k_dec, (((0,), (0,)), ((), ())), preferred_element_type=F32)
            inv = lax.rsqrt(jnp.mean(o * o, axis=-1, keepdims=True) + EPS)
            o = (o * inv) * gain
            o_ref[rows, col[hh]] = (o * _silu(zg_ref[rows, col[hh]])).astype(o_ref.dtype)
        return carry

    lax.fori_loop(0, n_chunks, chunk_body, 0)


def hgrn2(proj, lb_logits, gnorm_gain, batch, seq_len, layer, heads_per_block=8, rows=512):
    m, d4 = proj.shape
    d = d4 // 4
    heads = d // HEAD_DIM
    hb = heads_per_block if heads % heads_per_block == 0 else 1
    wblk = hb * HEAD_DIM
    nh = heads // hb
    rows = _tile(seq_len, rows, HG_CHUNK)
    ns = seq_len // rows
    cum, masks, signs = _hgrn_constants()
    depth = lb_logits.shape[0]

    def zspec(section):
        return pl.BlockSpec((rows, wblk), lambda b, h, s: (b * ns + s, section * nh + h))

    return pl.pallas_call(
        functools.partial(_hgrn_kernel, layer=layer, heads_per_block=hb),
        grid=(batch, nh, ns),
        in_specs=[pl.BlockSpec(cum.shape, lambda b, h, s: (0, 0)),
                  pl.BlockSpec(masks.shape, lambda b, h, s: (0, 0, 0)),
                  pl.BlockSpec(signs.shape, lambda b, h, s: (0, 0, 0)),
                  pl.BlockSpec((depth, wblk), lambda b, h, s: (0, h)),
                  pl.BlockSpec((1, HEAD_DIM), lambda b, h, s: (0, 0)),
                  zspec(0), zspec(1), zspec(2), zspec(3)],
        out_specs=pl.BlockSpec((rows, wblk), lambda b, h, s: (b * ns + s, h)),
        out_shape=jax.ShapeDtypeStruct((m, d), BF16),
        scratch_shapes=[pltpu.VMEM((hb, HEAD_DIM, HEAD_DIM), F32)],
        compiler_params=_params("parallel", "parallel", "arbitrary"),
        name="hgrn2",
    )(jnp.asarray(cum, BF16), jnp.asarray(masks), jnp.asarray(signs), lb_logits, gnorm_gain.reshape(1, HEAD_DIM),
      proj, proj, proj, proj)


def kernel(x, mix_norm, ffn_norm, final_norm, rel_bias, attn_w_in, attn_lambda, attn_subln,
           attn_w_out, hgrn_w_in, hgrn_lb_logits, hgrn_gnorm, hgrn_w_out, ffn_w_up, ffn_conv_w,
           ffn_conv_b, ffn_w_down):
    batch, seq_len, d = x.shape
    depth = mix_norm.shape[0]
    ffn_w_down = ffn_w_down.astype(BF16)
    h = x.reshape(batch * seq_len, d)
    for layer in range(depth):
        hn = rmsnorm(h, mix_norm[layer], BF16)
        if layer % 2 == 0:
            a = layer // 2
            qkv = matmul(hn, attn_w_in, a, BF16)
            o = diff_attention(qkv, attn_lambda[a], attn_subln[a], rel_bias, batch, seq_len, layer)
            h = matmul(o, attn_w_out, a, F32, residual=h)
        else:
            r = layer // 2
            proj = matmul(hn, hgrn_w_in, r, F32)
            o = hgrn2(proj, hgrn_lb_logits, hgrn_gnorm[r], batch, seq_len, layer)
            h = matmul(o, hgrn_w_out, r, F32, residual=h)
        hn = rmsnorm(h, ffn_norm[layer], BF16)
        act = ffn_up(hn, ffn_w_up, layer, ffn_conv_w[layer], ffn_conv_b[layer], seq_len)
        h = matmul(act, ffn_w_down, layer, F32, residual=h, tm=512, tn=512)
    return rmsnorm(h, final_norm, F32).reshape(batch, seq_len, d)
```

```python
import functools
import math

import numpy as np
import jax
import jax.numpy as jnp
from jax import lax
from jax.experimental import pallas as pl
from jax.experimental.pallas import tpu as pltpu

EPS = 1e-6
HEAD_DIM = 128
REL_BUCKETS = 32
REL_MAX_DIST = 128
HG_CHUNK = 64
CONV_W = 3
NEG_BIG = -1e30
REL_BLOCK = 128
LOG2E = math.log2(math.e)

V7X_VMEM_BYTES = 64 * 1024 * 1024
VMEM_LIMIT = V7X_VMEM_BYTES - 4 * 1024 * 1024
SUBLANES = 8
LANES = 128

F32 = jnp.float32
BF16 = jnp.bfloat16


def _params(*sem):
    return pltpu.CompilerParams(dimension_semantics=sem, vmem_limit_bytes=VMEM_LIMIT)


def _tile(dim, pref, align):
    if dim <= pref:
        return dim
    t = (pref // align) * align
    while t >= align:
        if dim % t == 0:
            return t
        t -= align
    return dim


def _silu(x):
    hx = 0.5 * x
    return hx + hx * jnp.tanh(hx)


def _rmsnorm_kernel(x_ref, g_ref, o_ref):
    x = x_ref[...]
    inv = lax.rsqrt(jnp.mean(x * x, axis=-1, keepdims=True) + EPS)
    o_ref[...] = ((x * inv) * g_ref[...]).astype(o_ref.dtype)


def rmsnorm(x, gain, out_dtype, tm=512):
    m, d = x.shape
    tm = _tile(m, tm, SUBLANES)
    return pl.pallas_call(
        _rmsnorm_kernel,
        grid=(m // tm,),
        in_specs=[pl.BlockSpec((tm, d), lambda i: (i, 0)),
                  pl.BlockSpec((1, d), lambda i: (0, 0))],
        out_specs=pl.BlockSpec((tm, d), lambda i: (i, 0)),
        out_shape=jax.ShapeDtypeStruct((m, d), out_dtype),
        compiler_params=_params("parallel"),
        name="rmsnorm",
    )(x, gain.reshape(1, d))


def _matmul_kernel(x_ref, w_ref, *rest, has_res, cast_w):
    rest = list(rest)
    r_ref = rest.pop(0) if has_res else None
    o_ref = rest.pop(0)
    if cast_w:
        w_sc, = rest

        @pl.when(pl.program_id(1) == 0)
        def _():
            w_sc[...] = w_ref[...].astype(BF16)

        w = w_sc[...]
    else:
        w = w_ref[...]
    acc = jnp.dot(x_ref[...], w, preferred_element_type=F32)
    if has_res:
        acc = r_ref[...] + acc
    o_ref[...] = acc.astype(o_ref.dtype)


def matmul(x, w, layer, out_dtype, residual=None, tm=512, tn=1024):
    m, kdim = x.shape
    n = w.shape[2]
    tm = _tile(m, tm, SUBLANES)
    tn = _tile(n, tn, LANES)
    has_res = residual is not None
    cast_w = w.dtype != BF16
    in_specs = [pl.BlockSpec((tm, kdim), lambda j, i: (i, 0)),
                pl.BlockSpec((None, kdim, tn), lambda j, i: (layer, 0, j))]
    args = [x, w]
    if has_res:
        in_specs.append(pl.BlockSpec((tm, tn), lambda j, i: (i, j)))
        args.append(residual)
    return pl.pallas_call(
        functools.partial(_matmul_kernel, has_res=has_res, cast_w=cast_w),
        grid=(n // tn, m // tm),
        in_specs=in_specs,
        out_specs=pl.BlockSpec((tm, tn), lambda j, i: (i, j)),
        out_shape=jax.ShapeDtypeStruct((m, n), out_dtype),
        scratch_shapes=[pltpu.VMEM((kdim, tn), BF16)] if cast_w else [],
        compiler_params=_params("parallel", "arbitrary"),
        name="matmul_res" if has_res else "matmul",
    )(*args)


def _ffn_up_kernel(x_ref, wg_ref, wv_ref, cw_ref, cb_ref, o_ref, w_sc, tail_sc, *,
                   tiles_per_seq, n_sub):
    i = pl.program_id(1)
    tm, tn = o_ref.shape
    sub = tm // n_sub

    @pl.when(i == 0)
    def _():
        w_sc[:, :tn] = wg_ref[...].astype(BF16)
        w_sc[:, tn:] = wv_ref[...].astype(BF16)

    @pl.when((i % tiles_per_seq) == 0)
    def _():
        tail_sc[...] = jnp.zeros(tail_sc.shape, F32)

    cw = cw_ref[...]
    cb = cb_ref[...]
    row = lax.broadcasted_iota(jnp.int32, (sub, tn), 0)
    for c in range(n_sub):
        r0 = c * sub
        up = jnp.dot(x_ref[r0:r0 + sub, :], w_sc[...], preferred_element_type=F32)
        gate = up[:, :tn]
        val = up[:, tn:]
        prev1 = tail_sc[SUBLANES - 1:SUBLANES, :]
        prev2 = tail_sc[SUBLANES - 2:SUBLANES - 1, :]
        tail_sc[...] = gate[sub - SUBLANES:, :]
        g1 = jnp.where(row >= 1, pltpu.roll(gate, 1, 0), prev1)
        g2 = jnp.where(row >= 2, pltpu.roll(gate, 2, 0), jnp.where(row == 1, prev1, prev2))
        conv = cb + g2 * cw[0:1, :]
        conv = conv + g1 * cw[1:2, :]
        conv = conv + gate * cw[2:3, :]
        o_ref[r0:r0 + sub, :] = (_silu(conv) * val).astype(o_ref.dtype)


def ffn_up(x, w_up, layer, conv_w, conv_b, seq_len, tm=2048, tn=256, n_sub=4):
    m, d = x.shape
    f = conv_w.shape[1]
    tn = _tile(f, tn, LANES)
    tm = _tile(seq_len, tm, SUBLANES)
    nj = f // tn
    return pl.pallas_call(
        functools.partial(_ffn_up_kernel, tiles_per_seq=seq_len // tm, n_sub=n_sub),
        grid=(nj, m // tm),
        in_specs=[pl.BlockSpec((tm, d), lambda j, i: (i, 0)),
                  pl.BlockSpec((None, d, tn), lambda j, i: (layer, 0, j)),
                  pl.BlockSpec((None, d, tn), lambda j, i: (layer, 0, nj + j)),
                  pl.BlockSpec((CONV_W, tn), lambda j, i: (0, j)),
                  pl.BlockSpec((1, tn), lambda j, i: (0, j))],
        out_specs=pl.BlockSpec((tm, tn), lambda j, i: (i, j)),
        out_shape=jax.ShapeDtypeStruct((m, f), BF16),
        scratch_shapes=[pltpu.VMEM((d, 2 * tn), BF16),
                        pltpu.VMEM((SUBLANES, tn), F32)],
        compiler_params=_params("arbitrary", "arbitrary"),
        name="ffn_up",
    )(x, w_up, w_up, conv_w, conv_b.reshape(1, f))


def _lane_tile(x, reps):
    return x if reps == 1 else jnp.concatenate([x] * reps, axis=1)


def _lane_fold(x):
    out = x[:, :LANES]
    for b in range(1, x.shape[1] // LANES):
        out = out + x[:, b * LANES:(b + 1) * LANES]
    return out


def _attn_kernel(lam_ref, gain_ref, d0_ref, d1_ref, q_ref, k_ref, v_ref, o_ref,
                 bd_sc, bs_sc, q_sc, s_sc, m_sc, l_sc, acc_sc, *, lam_init, t):
    dh = HEAD_DIM
    nb = t // REL_BLOCK

    blocks = {0: d0_ref[0], 1: d1_ref[0]}
    zero = jnp.zeros((REL_BLOCK, REL_BLOCK), F32)
    neg = jnp.full((REL_BLOCK, REL_BLOCK), NEG_BIG, F32)
    for r in range(nb):
        for c in range(nb):
            blk = blocks.get(r - c, zero if r > c else neg)
            bd_sc[r * REL_BLOCK:(r + 1) * REL_BLOCK, c * REL_BLOCK:(c + 1) * REL_BLOCK] = blk
    bs_sc[...] = jnp.zeros(bs_sc.shape, F32)
    bs_sc[0:REL_BLOCK, t - REL_BLOCK:t] = blocks[1]

    def logits(j, slot):
        kt = k_ref[pl.ds(pl.multiple_of(j * t, t), t), :]
        for c in range(2):
            s_sc[slot, c] = lax.dot_general(q_sc[:, c * dh:(c + 1) * dh], kt[:, c * dh:(c + 1) * dh],
                                            (((1,), (1,)), ((), ())), preferred_element_type=F32)

    def softmax_pv(j, slot, bias_ref):
        vt = v_ref[pl.ds(pl.multiple_of(j * t, t), t), :]
        for c in range(2):
            s = s_sc[slot, c]
            if bias_ref is not None:
                s = s + bias_ref[...]
            m_prev = m_sc[c]
            m_next = jnp.maximum(m_prev, jnp.max(s, axis=1, keepdims=True))
            p = jnp.exp2(s - _lane_tile(m_next, t // LANES))
            alpha = jnp.exp2(m_prev - m_next)
            p = p.astype(BF16)
            l_sc[c] = alpha * l_sc[c] + _lane_fold(p.astype(F32))
            m_sc[c] = m_next
            pv = jnp.dot(p, vt, preferred_element_type=F32)
            acc_sc[c] = acc_sc[c] * _lane_tile(alpha, 2 * dh // LANES) + pv

    def q_tile(i, carry0):
        rows = pl.ds(pl.multiple_of(i * t, t), t)
        q_sc[...] = (q_ref[rows, :].astype(F32) * (dh ** -0.5 * LOG2E)).astype(BF16)
        m_sc[...] = jnp.full(m_sc.shape, NEG_BIG, F32)
        l_sc[...] = jnp.zeros(l_sc.shape, F32)
        acc_sc[...] = jnp.zeros(acc_sc.shape, F32)

        n_far = jnp.maximum(i - 1, 0)
        odd = n_far % 2

        @pl.when(odd == 0)
        def _():
            logits(0, 0)

        @pl.when(odd == 1)
        def _():
            logits(0, 1)
            logits(1, 0)
            softmax_pv(0, 1, None)

        def pair_body(pi, carry):
            j = odd + 2 * pi
            logits(j + 1, 1)
            softmax_pv(j, 0, None)
            logits(j + 2, 0)
            softmax_pv(j + 1, 1, None)
            return carry

        n_pairs = n_far // 2

        def quad_body(qi, carry):
            pair_body(2 * qi, carry)
            pair_body(2 * qi + 1, carry)
            return carry

        lax.fori_loop(0, n_pairs // 2, quad_body, 0)

        @pl.when(n_pairs % 2 == 1)
        def _():
            pair_body(n_pairs - 1, 0)

        @pl.when(i == 0)
        def _():
            softmax_pv(0, 0, bd_sc)

        @pl.when(i >= 1)
        def _():
            logits(i, 1)
            softmax_pv(i - 1, 0, bs_sc)
            softmax_pv(i, 1, bd_sc)

        o1 = acc_sc[0] / jnp.sum(l_sc[0], axis=1, keepdims=True)
        o2 = acc_sc[1] / jnp.sum(l_sc[1], axis=1, keepdims=True)
        lp = lam_ref[...]
        lam = (jnp.exp(jnp.sum(lp[0:1] * lp[1:2], axis=1, keepdims=True))
               - jnp.exp(jnp.sum(lp[2:3] * lp[3:4], axis=1, keepdims=True)) + lam_init)
        a = o1 - lam * o2
        inv = lax.rsqrt(jnp.mean(a * a, axis=-1, keepdims=True) + EPS)
        o_ref[rows, :] = (((a * inv) * gain_ref[...]) * (1.0 - lam_init)).astype(o_ref.dtype)
        return carry0

    lax.fori_loop(0, q_ref.shape[0] // t, q_tile, 0)


def _t5_causal_bucket(dist):
    n = jnp.maximum(dist, 0)
    max_exact = REL_BUCKETS // 2
    nf = jnp.maximum(n, 1).astype(F32)
    large = max_exact + (jnp.log(nf / max_exact) / math.log(REL_MAX_DIST / max_exact)
                         * (REL_BUCKETS - max_exact)).astype(jnp.int32)
    large = jnp.minimum(large, REL_BUCKETS - 1)
    return jnp.where(n < max_exact, n, large)


def _bias_blocks(rel_bias):
    rel_by_dist = rel_bias[_t5_causal_bucket(jnp.arange(2 * REL_BLOCK))]
    table = ((rel_by_dist - rel_bias[REL_BUCKETS - 1][None, :]) * LOG2E).T
    r = np.arange(REL_BLOCK)[:, None]
    c = np.arange(REL_BLOCK)[None, :]
    d0 = jnp.where(jnp.asarray(r >= c)[None], table[:, np.maximum(r - c, 0)], NEG_BIG)
    d1 = table[:, REL_BLOCK + r - c]
    return d0, d1


def diff_attention(qkv, lam_params, subln_gain, rel_bias, batch, seq_len, layer_idx, t=512):
    m, d3 = qkv.shape
    d = d3 // 3
    dv = 2 * HEAD_DIM
    heads = d // dv
    t = _tile(seq_len, t, REL_BLOCK)
    lam_init = 0.8 - 0.6 * math.exp(-0.3 * layer_idx)
    d0, d1 = _bias_blocks(rel_bias)
    blk = (1, REL_BLOCK, REL_BLOCK)
    return pl.pallas_call(
        functools.partial(_attn_kernel, lam_init=lam_init, t=t),
        grid=(batch, heads),
        in_specs=[pl.BlockSpec((4, HEAD_DIM), lambda b, h: (0, 0)),
                  pl.BlockSpec((1, dv), lambda b, h: (0, 0)),
                  pl.BlockSpec(blk, lambda b, h: (h, 0, 0)),
                  pl.BlockSpec(blk, lambda b, h: (h, 0, 0)),
                  pl.BlockSpec((seq_len, dv), lambda b, h: (b, h)),
                  pl.BlockSpec((seq_len, dv), lambda b, h: (b, heads + h)),
                  pl.BlockSpec((seq_len, dv), lambda b, h: (b, 2 * heads + h))],
        out_specs=pl.BlockSpec((seq_len, dv), lambda b, h: (b, h)),
        out_shape=jax.ShapeDtypeStruct((m, d), BF16),
        scratch_shapes=[pltpu.VMEM((t, t), F32),
                        pltpu.VMEM((t, t), F32),
                        pltpu.VMEM((t, dv), BF16),
                        pltpu.VMEM((2, 2, t, t), F32),
                        pltpu.VMEM((2, t, LANES), F32),
                        pltpu.VMEM((2, t, LANES), F32),
                        pltpu.VMEM((2, t, dv), F32)],
        compiler_params=_params("parallel", "parallel"),
        name="diff_attention",
    )(lam_params, subln_gain.reshape(1, dv), d0, d1, qkv, qkv, qkv)


def _hgrn_levels():
    return [HG_CHUNK >> l for l in range(int(math.log2(HG_CHUNK)) - 1)]


def _hgrn_constants():
    c = HG_CHUNK
    t = np.arange(c)[:, None]
    s = np.arange(c)[None, :]
    tril = (s <= t).astype(np.float32)
    masks = []
    for n in _hgrn_levels() + [2]:
        same = (t // n) == (s // n)
        masks.append(same & (t % n >= n // 2) & (s % n < n // 2))
    masks.append(t == s)
    signs = [np.where(t % n >= n // 2, 1.0, -1.0) * np.ones((1, HEAD_DIM)) for n in _hgrn_levels()]
    return tril, np.stack(masks).astype(np.float32), np.stack(signs).astype(np.float32)


def _level_reference(g, n, low_half):
    def row(r):
        return jnp.broadcast_to(g[r:r + 1, :], (SUBLANES, g.shape[1]))
    groups = []
    for gi in range(g.shape[0] // SUBLANES):
        base = gi * SUBLANES
        if n >= SUBLANES:
            groups.append(row((base // n) * n + n // 2 - 1))
        else:
            groups.append(jnp.where(low_half, row(base + 1), row(base + 5)))
    return jnp.concatenate(groups, axis=0)


def _hgrn_kernel(cum_ref, mask_ref, sign_ref, lbl_ref, gn_ref, zq_ref, zf_ref, vi_ref, zg_ref, o_ref,
                 state_sc, *, layer, heads_per_block):
    c = HG_CHUNK
    dk = HEAD_DIM
    n_lvl = len(_hgrn_levels())

    @pl.when(pl.program_id(2) == 0)
    def _():
        state_sc[...] = jnp.zeros(state_sc.shape, F32)

    lg = lbl_ref[...]
    e = jnp.exp(lg - jnp.max(lg, axis=0, keepdims=True))
    p = e / jnp.sum(e, axis=0, keepdims=True)
    lb_all = jnp.sum(p[1:layer + 1], axis=0, keepdims=True)
    f_mid = 0.5 * (1.0 + lb_all)
    f_amp = 0.5 * (1.0 - lb_all)

    cum = cum_ref[...]
    gain = gn_ref[...]
    odd_row = (lax.broadcasted_iota(jnp.int32, (c, dk), 0) % 2) == 1
    low_half = (lax.broadcasted_iota(jnp.int32, (SUBLANES, dk), 0) % SUBLANES) < SUBLANES // 2
    levels = _hgrn_levels()
    n_chunks = zq_ref.shape[0] // c

    def chunk_body(ci, carry):
        rows = pl.ds(pl.multiple_of(ci * c, c), c)
        heads = range(heads_per_block)
        col = [slice(hh * dk, (hh + 1) * dk) for hh in heads]

        q, f, k, v, cs = [], [], [], [], []
        for hh in heads:
            q.append(_silu(zq_ref[rows, col[hh]]))
            fh = f_mid[:, col[hh]] + f_amp[:, col[hh]] * jnp.tanh(0.5 * zf_ref[rows, col[hh]])
            f.append(fh)
            k.append(1.0 - fh)
            v.append(vi_ref[rows, col[hh]].astype(BF16))
            logf = jnp.log2(fh)
            hi = logf.astype(BF16)
            r1 = logf - hi.astype(F32)
            mid = r1.astype(BF16)
            lo = (r1 - mid.astype(F32)).astype(BF16)
            x = jnp.dot(cum, jnp.concatenate([hi, mid, lo], axis=1),
                        preferred_element_type=F32)
            cs.append((x[:, :dk] + x[:, dk:2 * dk]) + x[:, 2 * dk:])

        scores = []
        for hh in heads:
            sc = None
            for l in range(n_lvl + 2):
                if l < n_lvl:
                    d = cs[hh] - _level_reference(cs[hh], levels[l], low_half)
                    w = jnp.exp2(d * sign_ref[l])
                    ql, kl = q[hh] * w, k[hh] * w
                elif l == n_lvl:
                    ql, kl = q[hh] * jnp.where(odd_row, f[hh], 1.0), k[hh]
                else:
                    ql, kl = q[hh], k[hh]
                s_l = lax.dot_general(ql.astype(BF16), kl.astype(BF16),
                                      (((1,), (1,)), ((), ())), preferred_element_type=F32)
                s_l = s_l * mask_ref[l]
                sc = s_l if sc is None else sc + s_l
            scores.append(sc)

        for hh in heads:
            g = cs[hh]
            state_t = state_sc[hh]
            o = lax.dot_general((q[hh] * jnp.exp2(g)).astype(BF16), state_t.astype(BF16),
                                (((1,), (1,)), ((), ())), preferred_element_type=F32)
            o = o + jnp.dot(scores[hh].astype(BF16), v[hh], preferred_element_type=F32)
            g_last = g[c - 1:c]
            k_dec = (k[hh] * jnp.exp2(g_last - g)).astype(BF16)
            state_sc[hh] = state_t * jnp.exp2(g_last) + lax.dot_general(
                v[hh], k_dec, (((0,), (0,)), ((), ())), preferred_element_type=F32)
            inv = lax.rsqrt(jnp.mean(o * o, axis=-1, keepdims=True) + EPS)
            o = (o * inv) * gain
            o_ref[rows, col[hh]] = (o * _silu(zg_ref[rows, col[hh]])).astype(o_ref.dtype)
        return carry

    lax.fori_loop(0, n_chunks, chunk_body, 0)


def hgrn2(proj, lb_logits, gnorm_gain, batch, seq_len, layer, heads_per_block=8, rows=512):
    m, d4 = proj.shape
    d = d4 // 4
    heads = d // HEAD_DIM
    hb = heads_per_block if heads % heads_per_block == 0 else 1
    wblk = hb * HEAD_DIM
    nh = heads // hb
    rows = _tile(seq_len, rows, HG_CHUNK)
    ns = seq_len // rows
    cum, masks, signs = _hgrn_constants()
    depth = lb_logits.shape[0]

    def zspec(section):
        return pl.BlockSpec((rows, wblk), lambda b, h, s: (b * ns + s, section * nh + h))

    return pl.pallas_call(
        functools.partial(_hgrn_kernel, layer=layer, heads_per_block=hb),
        grid=(batch, nh, ns),
        in_specs=[pl.BlockSpec(cum.shape, lambda b, h, s: (0, 0)),
                  pl.BlockSpec(masks.shape, lambda b, h, s: (0, 0, 0)),
                  pl.BlockSpec(signs.shape, lambda b, h, s: (0, 0, 0)),
                  pl.BlockSpec((depth, wblk), lambda b, h, s: (0, h)),
                  pl.BlockSpec((1, HEAD_DIM), lambda b, h, s: (0, 0)),
                  zspec(0), zspec(1), zspec(2), zspec(3)],
        out_specs=pl.BlockSpec((rows, wblk), lambda b, h, s: (b * ns + s, h)),
        out_shape=jax.ShapeDtypeStruct((m, d), BF16),
        scratch_shapes=[pltpu.VMEM((hb, HEAD_DIM, HEAD_DIM), F32)],
        compiler_params=_params("parallel", "parallel", "arbitrary"),
        name="hgrn2",
    )(jnp.asarray(cum, BF16), jnp.asarray(masks), jnp.asarray(signs), lb_logits, gnorm_gain.reshape(1, HEAD_DIM),
      proj, proj, proj, proj)


def kernel(x, mix_norm, ffn_norm, final_norm, rel_bias, attn_w_in, attn_lambda, attn_subln,
           attn_w_out, hgrn_w_in, hgrn_lb_logits, hgrn_gnorm, hgrn_w_out, ffn_w_up, ffn_conv_w,
           ffn_conv_b, ffn_w_down):
    batch, seq_len, d = x.shape
    depth = mix_norm.shape[0]
    ffn_w_down = ffn_w_down.astype(BF16)
    h = x.reshape(batch * seq_len, d)
    for layer in range(depth):
        hn = rmsnorm(h, mix_norm[layer], BF16)
        if layer % 2 == 0:
            a = layer // 2
            qkv = matmul(hn, attn_w_in, a, BF16)
            o = diff_attention(qkv, attn_lambda[a], attn_subln[a], rel_bias, batch, seq_len, layer)
            h = matmul(o, attn_w_out, a, F32, residual=h)
        else:
            r = layer // 2
            proj = matmul(hn, hgrn_w_in, r, F32)
            o = hgrn2(proj, hgrn_lb_logits, hgrn_gnorm[r], batch, seq_len, layer,
                      rows=1024 if r == 0 else 512)
            h = matmul(o, hgrn_w_out, r, F32, residual=h)
        hn = rmsnorm(h, ffn_norm[layer], BF16)
        act = ffn_up(hn, ffn_w_up, layer, ffn_conv_w[layer], ffn_conv_b[layer], seq_len)
        h = matmul(act, ffn_w_down, layer, F32, residual=h, tm=512, tn=512)
    return rmsnorm(h, final_norm, F32).reshape(batch, seq_len, d)
```

```python
import functools
import math

import numpy as np
import jax
import jax.numpy as jnp
from jax import lax
from jax.experimental import pallas as pl
from jax.experimental.pallas import tpu as pltpu

EPS = 1e-6
HEAD_DIM = 128
REL_BUCKETS = 32
REL_MAX_DIST = 128
HG_CHUNK = 64
CONV_W = 3
NEG_BIG = -1e30
REL_BLOCK = 128
LOG2E = math.log2(math.e)

V7X_VMEM_BYTES = 64 * 1024 * 1024
VMEM_LIMIT = V7X_VMEM_BYTES - 4 * 1024 * 1024
SUBLANES = 8
LANES = 128

F32 = jnp.float32
BF16 = jnp.bfloat16


def _params(*sem):
    return pltpu.CompilerParams(dimension_semantics=sem, vmem_limit_bytes=VMEM_LIMIT)


def _tile(dim, pref, align):
    if dim <= pref:
        return dim
    t = (pref // align) * align
    while t >= align:
        if dim % t == 0:
            return t
        t -= align
    return dim


def _silu(x):
    hx = 0.5 * x
    return hx + hx * jnp.tanh(hx)


def _rmsnorm_kernel(x_ref, g_ref, o_ref):
    x = x_ref[...]
    inv = lax.rsqrt(jnp.mean(x * x, axis=-1, keepdims=True) + EPS)
    o_ref[...] = ((x * inv) * g_ref[...]).astype(o_ref.dtype)


def rmsnorm(x, gain, out_dtype, tm=512):
    m, d = x.shape
    tm = _tile(m, tm, SUBLANES)
    return pl.pallas_call(
        _rmsnorm_kernel,
        grid=(m // tm,),
        in_specs=[pl.BlockSpec((tm, d), lambda i: (i, 0)),
                  pl.BlockSpec((1, d), lambda i: (0, 0))],
        out_specs=pl.BlockSpec((tm, d), lambda i: (i, 0)),
        out_shape=jax.ShapeDtypeStruct((m, d), out_dtype),
        compiler_params=_params("parallel"),
        name="rmsnorm",
    )(x, gain.reshape(1, d))


def _matmul_kernel(x_ref, w_ref, *rest, has_res, cast_w):
    rest = list(rest)
    r_ref = rest.pop(0) if has_res else None
    o_ref = rest.pop(0)
    if cast_w:
        w_sc, = rest

        @pl.when(pl.program_id(1) == 0)
        def _():
            w_sc[...] = w_ref[...].astype(BF16)

        w = w_sc[...]
    else:
        w = w_ref[...]
    acc = jnp.dot(x_ref[...], w, preferred_element_type=F32)
    if has_res:
        acc = r_ref[...] + acc
    o_ref[...] = acc.astype(o_ref.dtype)


def matmul(x, w, layer, out_dtype, residual=None, tm=512, tn=1024):
    m, kdim = x.shape
    n = w.shape[2]
    tm = _tile(m, tm, SUBLANES)
    tn = _tile(n, tn, LANES)
    has_res = residual is not None
    cast_w = w.dtype != BF16
    in_specs = [pl.BlockSpec((tm, kdim), lambda j, i: (i, 0)),
                pl.BlockSpec((None, kdim, tn), lambda j, i: (layer, 0, j))]
    args = [x, w]
    if has_res:
        in_specs.append(pl.BlockSpec((tm, tn), lambda j, i: (i, j)))
        args.append(residual)
    return pl.pallas_call(
        functools.partial(_matmul_kernel, has_res=has_res, cast_w=cast_w),
        grid=(n // tn, m // tm),
        in_specs=in_specs,
        out_specs=pl.BlockSpec((tm, tn), lambda j, i: (i, j)),
        out_shape=jax.ShapeDtypeStruct((m, n), out_dtype),
        scratch_shapes=[pltpu.VMEM((kdim, tn), BF16)] if cast_w else [],
        compiler_params=_params("parallel", "arbitrary"),
        name="matmul_res" if has_res else "matmul",
    )(*args)


def _ffn_up_kernel(x_ref, wg_ref, wv_ref, cw_ref, cb_ref, o_ref, w_sc, tail_sc, *,
                   tiles_per_seq, n_sub):
    i = pl.program_id(1)
    tm, tn = o_ref.shape
    sub = tm // n_sub

    @pl.when(i == 0)
    def _():
        w_sc[:, :tn] = wg_ref[...].astype(BF16)
        w_sc[:, tn:] = wv_ref[...].astype(BF16)

    @pl.when((i % tiles_per_seq) == 0)
    def _():
        tail_sc[...] = jnp.zeros(tail_sc.shape, F32)

    cw = cw_ref[...]
    cb = cb_ref[...]
    row = lax.broadcasted_iota(jnp.int32, (sub, tn), 0)
    for c in range(n_sub):
        r0 = c * sub
        up = jnp.dot(x_ref[r0:r0 + sub, :], w_sc[...], preferred_element_type=F32)
        gate = up[:, :tn]
        val = up[:, tn:]
        prev1 = tail_sc[SUBLANES - 1:SUBLANES, :]
        prev2 = tail_sc[SUBLANES - 2:SUBLANES - 1, :]
        tail_sc[...] = gate[sub - SUBLANES:, :]
        g1 = jnp.where(row >= 1, pltpu.roll(gate, 1, 0), prev1)
        g2 = jnp.where(row >= 2, pltpu.roll(gate, 2, 0), jnp.where(row == 1, prev1, prev2))
        conv = cb + g2 * cw[0:1, :]
        conv = conv + g1 * cw[1:2, :]
        conv = conv + gate * cw[2:3, :]
        o_ref[r0:r0 + sub, :] = (_silu(conv) * val).astype(o_ref.dtype)


def ffn_up(x, w_up, layer, conv_w, conv_b, seq_len, tm=2048, tn=256, n_sub=4):
    m, d = x.shape
    f = conv_w.shape[1]
    tn = _tile(f, tn, LANES)
    tm = _tile(seq_len, tm, SUBLANES)
    nj = f // tn
    return pl.pallas_call(
        functools.partial(_ffn_up_kernel, tiles_per_seq=seq_len // tm, n_sub=n_sub),
        grid=(nj, m // tm),
        in_specs=[pl.BlockSpec((tm, d), lambda j, i: (i, 0)),
                  pl.BlockSpec((None, d, tn), lambda j, i: (layer, 0, j)),
                  pl.BlockSpec((None, d, tn), lambda j, i: (layer, 0, nj + j)),
                  pl.BlockSpec((CONV_W, tn), lambda j, i: (0, j)),
                  pl.BlockSpec((1, tn), lambda j, i: (0, j))],
        out_specs=pl.BlockSpec((tm, tn), lambda j, i: (i, j)),
        out_shape=jax.ShapeDtypeStruct((m, f), BF16),
        scratch_shapes=[pltpu.VMEM((d, 2 * tn), BF16),
                        pltpu.VMEM((SUBLANES, tn), F32)],
        compiler_params=_params("arbitrary", "arbitrary"),
        name="ffn_up",
    )(x, w_up, w_up, conv_w, conv_b.reshape(1, f))


def _lane_tile(x, reps):
    return x if reps == 1 else jnp.concatenate([x] * reps, axis=1)


def _lane_fold(x):
    out = x[:, :LANES]
    for b in range(1, x.shape[1] // LANES):
        out = out + x[:, b * LANES:(b + 1) * LANES]
    return out


def _attn_kernel(lam_ref, gain_ref, d0_ref, d1_ref, q_ref, k_ref, v_ref, o_ref,
                 bd_sc, q_sc, s_sc, m_sc, l_sc, acc_sc, *, lam_init, t):
    dh = HEAD_DIM
    nb = t // REL_BLOCK

    blocks = {0: d0_ref[0], 1: d1_ref[0]}
    zero = jnp.zeros((REL_BLOCK, REL_BLOCK), F32)
    neg = jnp.full((REL_BLOCK, REL_BLOCK), NEG_BIG, F32)
    for r in range(nb):
        for c in range(nb):
            blk = blocks.get(r - c, zero if r > c else neg)
            bd_sc[r * REL_BLOCK:(r + 1) * REL_BLOCK, c * REL_BLOCK:(c + 1) * REL_BLOCK] = blk

    def logits(j, slot):
        kt = k_ref[pl.ds(pl.multiple_of(j * t, t), t), :]
        for c in range(2):
            s_sc[slot, c] = lax.dot_general(q_sc[:, c * dh:(c + 1) * dh], kt[:, c * dh:(c + 1) * dh],
                                            (((1,), (1,)), ((), ())), preferred_element_type=F32)

    def softmax_pv(j, slot, bias_ref):
        vt = v_ref[pl.ds(pl.multiple_of(j * t, t), t), :]
        for c in range(2):
            s = s_sc[slot, c]
            if bias_ref is not None:
                s = s + bias_ref[...]
            m_prev = m_sc[c]
            m_next = jnp.maximum(m_prev, jnp.max(s, axis=1, keepdims=True))
            p = jnp.exp2(s - _lane_tile(m_next, t // LANES))
            alpha = jnp.exp2(m_prev - m_next)
            p = p.astype(BF16)
            l_sc[c] = alpha * l_sc[c] + _lane_fold(p.astype(F32))
            m_sc[c] = m_next
            pv = jnp.dot(p, vt, preferred_element_type=F32)
            acc_sc[c] = acc_sc[c] * _lane_tile(alpha, 2 * dh // LANES) + pv

    def q_tile(i, carry0):
        rows = pl.ds(pl.multiple_of(i * t, t), t)

        def init():
            q_sc[...] = (q_ref[rows, :].astype(F32) * (dh ** -0.5 * LOG2E)).astype(BF16)
            m_sc[...] = jnp.full(m_sc.shape, NEG_BIG, F32)
            l_sc[...] = jnp.zeros(l_sc.shape, F32)
            acc_sc[...] = jnp.zeros(acc_sc.shape, F32)

        n_far = jnp.maximum(i - 1, 0)
        odd = n_far % 2

        @pl.when(odd == 0)
        def _():
            init()
            logits(0, 0)

        @pl.when(odd == 1)
        def _():
            init()
            logits(0, 1)
            logits(1, 0)
            softmax_pv(0, 1, None)

        def pair_body(pi, carry):
            j = odd + 2 * pi
            logits(j + 1, 1)
            softmax_pv(j, 0, None)
            logits(j + 2, 0)
            softmax_pv(j + 1, 1, None)
            return carry

        n_pairs = n_far // 2

        def quad_body(qi, carry):
            pair_body(2 * qi, carry)
            pair_body(2 * qi + 1, carry)
            return carry

        lax.fori_loop(0, n_pairs // 2, quad_body, 0)

        @pl.when(n_pairs % 2 == 1)
        def _():
            pair_body(n_pairs - 1, 0)

        def finalize():
            o1 = acc_sc[0] / jnp.sum(l_sc[0], axis=1, keepdims=True)
            o2 = acc_sc[1] / jnp.sum(l_sc[1], axis=1, keepdims=True)
            lp = lam_ref[...]
            lam = (jnp.exp(jnp.sum(lp[0:1] * lp[1:2], axis=1, keepdims=True))
                   - jnp.exp(jnp.sum(lp[2:3] * lp[3:4], axis=1, keepdims=True)) + lam_init)
            a = o1 - lam * o2
            inv = lax.rsqrt(jnp.mean(a * a, axis=-1, keepdims=True) + EPS)
            o_ref[rows, :] = (((a * inv) * gain_ref[...]) * (1.0 - lam_init)).astype(o_ref.dtype)

        @pl.when(i == 0)
        def _():
            softmax_pv(0, 0, bd_sc)
            finalize()

        @pl.when(i >= 1)
        def _():
            logits(i, 1)
            for c in range(2):
                s_sc[0, c, 0:REL_BLOCK, t - REL_BLOCK:t] = (
                    s_sc[0, c, 0:REL_BLOCK, t - REL_BLOCK:t] + blocks[1])
            softmax_pv(i - 1, 0, None)
            softmax_pv(i, 1, bd_sc)
            finalize()

        return carry0

    lax.fori_loop(0, q_ref.shape[0] // t, q_tile, 0)


def _t5_causal_bucket(dist):
    n = jnp.maximum(dist, 0)
    max_exact = REL_BUCKETS // 2
    nf = jnp.maximum(n, 1).astype(F32)
    large = max_exact + (jnp.log(nf / max_exact) / math.log(REL_MAX_DIST / max_exact)
                         * (REL_BUCKETS - max_exact)).astype(jnp.int32)
    large = jnp.minimum(large, REL_BUCKETS - 1)
    return jnp.where(n < max_exact, n, large)


def _bias_blocks(rel_bias):
    rel_by_dist = rel_bias[_t5_causal_bucket(jnp.arange(2 * REL_BLOCK))]
    table = ((rel_by_dist - rel_bias[REL_BUCKETS - 1][None, :]) * LOG2E).T
    r = np.arange(REL_BLOCK)[:, None]
    c = np.arange(REL_BLOCK)[None, :]
    d0 = jnp.where(jnp.asarray(r >= c)[None], table[:, np.maximum(r - c, 0)], NEG_BIG)
    d1 = table[:, REL_BLOCK + r - c]
    return d0, d1


def diff_attention(qkv, lam_params, subln_gain, rel_bias, batch, seq_len, layer_idx, t=512):
    m, d3 = qkv.shape
    d = d3 // 3
    dv = 2 * HEAD_DIM
    heads = d // dv
    t = _tile(seq_len, t, REL_BLOCK)
    lam_init = 0.8 - 0.6 * math.exp(-0.3 * layer_idx)
    d0, d1 = _bias_blocks(rel_bias)
    blk = (1, REL_BLOCK, REL_BLOCK)
    return pl.pallas_call(
        functools.partial(_attn_kernel, lam_init=lam_init, t=t),
        grid=(batch, heads),
        in_specs=[pl.BlockSpec((4, HEAD_DIM), lambda b, h: (0, 0)),
                  pl.BlockSpec((1, dv), lambda b, h: (0, 0)),
                  pl.BlockSpec(blk, lambda b, h: (h, 0, 0)),
                  pl.BlockSpec(blk, lambda b, h: (h, 0, 0)),
                  pl.BlockSpec((seq_len, dv), lambda b, h: (b, h)),
                  pl.BlockSpec((seq_len, dv), lambda b, h: (b, heads + h)),
                  pl.BlockSpec((seq_len, dv), lambda b, h: (b, 2 * heads + h))],
        out_specs=pl.BlockSpec((seq_len, dv), lambda b, h: (b, h)),
        out_shape=jax.ShapeDtypeStruct((m, d), BF16),
        scratch_shapes=[pltpu.VMEM((t, t), F32),
                        pltpu.VMEM((t, dv), BF16),
                        pltpu.VMEM((2, 2, t, t), F32),
                        pltpu.VMEM((2, t, LANES), F32),
                        pltpu.VMEM((2, t, LANES), F32),
                        pltpu.VMEM((2, t, dv), F32)],
        compiler_params=_params("parallel", "parallel"),
        name="diff_attention",
    )(lam_params, subln_gain.reshape(1, dv), d0, d1, qkv, qkv, qkv)


def _hgrn_levels():
    return [HG_CHUNK >> l for l in range(int(math.log2(HG_CHUNK)) - 1)]


def _hgrn_constants():
    c = HG_CHUNK
    t = np.arange(c)[:, None]
    s = np.arange(c)[None, :]
    tril = (s <= t).astype(np.float32)
    masks = []
    for n in _hgrn_levels() + [2]:
        same = (t // n) == (s // n)
        masks.append(same & (t % n >= n // 2) & (s % n < n // 2))
    masks.append(t == s)
    signs = [np.where(t % n >= n // 2, 1.0, -1.0) * np.ones((1, HEAD_DIM)) for n in _hgrn_levels()]
    return tril, np.stack(masks).astype(np.float32), np.stack(signs).astype(np.float32)


def _level_reference(g, n, low_half):
    def row(r):
        return jnp.broadcast_to(g[r:r + 1, :], (SUBLANES, g.shape[1]))
    groups = []
    for gi in range(g.shape[0] // SUBLANES):
        base = gi * SUBLANES
        if n >= SUBLANES:
            groups.append(row((base // n) * n + n // 2 - 1))
        else:
            groups.append(jnp.where(low_half, row(base + 1), row(base + 5)))
    return jnp.concatenate(groups, axis=0)


def _hgrn_kernel(cum_ref, mask_ref, sign_ref, lbl_ref, gn_ref, zq_ref, zf_ref, vi_ref, zg_ref, o_ref,
                 state_sc, *, layer, heads_per_block):
    c = HG_CHUNK
    dk = HEAD_DIM
    n_lvl = len(_hgrn_levels())

    @pl.when(pl.program_id(2) == 0)
    def _():
        state_sc[...] = jnp.zeros(state_sc.shape, F32)

    lg = lbl_ref[...]
    e = jnp.exp(lg - jnp.max(lg, axis=0, keepdims=True))
    p = e / jnp.sum(e, axis=0, keepdims=True)
    lb_all = jnp.sum(p[1:layer + 1], axis=0, keepdims=True)
    f_mid = 0.5 * (1.0 + lb_all)
    f_amp = 0.5 * (1.0 - lb_all)

    cum = cum_ref[...]
    gain = gn_ref[...]
    odd_row = (lax.broadcasted_iota(jnp.int32, (c, dk), 0) % 2) == 1
    low_half = (lax.broadcasted_iota(jnp.int32, (SUBLANES, dk), 0) % SUBLANES) < SUBLANES // 2
    levels = _hgrn_levels()
    n_chunks = zq_ref.shape[0] // c

    def chunk_body(ci, carry):
        rows = pl.ds(pl.multiple_of(ci * c, c), c)
        heads = range(heads_per_block)
        col = [slice(hh * dk, (hh + 1) * dk) for hh in heads]

        q, f, k, v, cs = [], [], [], [], []
        for hh in heads:
            q.append(_silu(zq_ref[rows, col[hh]]))
            fh = f_mid[:, col[hh]] + f_amp[:, col[hh]] * jnp.tanh(0.5 * zf_ref[rows, col[hh]])
            f.append(fh)
            k.append(1.0 - fh)
            v.append(vi_ref[rows, col[hh]].astype(BF16))
            logf = jnp.log2(fh)
            hi = logf.astype(BF16)
            r1 = logf - hi.astype(F32)
            mid = r1.astype(BF16)
            lo = (r1 - mid.astype(F32)).astype(BF16)
            x = jnp.dot(cum, jnp.concatenate([hi, mid, lo], axis=1),
                        preferred_element_type=F32)
            cs.append((x[:, :dk] + x[:, dk:2 * dk]) + x[:, 2 * dk:])

        scores = []
        for hh in heads:
            sc = None
            for l in range(n_lvl + 2):
                if l < n_lvl:
                    d = cs[hh] - _level_reference(cs[hh], levels[l], low_half)
                    w = jnp.exp2(d * sign_ref[l])
                    ql, kl = q[hh] * w, k[hh] * w
                elif l == n_lvl:
                    ql, kl = q[hh] * jnp.where(odd_row, f[hh], 1.0), k[hh]
                else:
                    ql, kl = q[hh], k[hh]
                s_l = lax.dot_general(ql.astype(BF16), kl.astype(BF16),
                                      (((1,), (1,)), ((), ())), preferred_element_type=F32)
                s_l = s_l * mask_ref[l]
                sc = s_l if sc is None else sc + s_l
            scores.append(sc)

        for hh in heads:
            g = cs[hh]
            state_t = state_sc[hh]
            o = lax.dot_general((q[hh] * jnp.exp2(g)).astype(BF16), state_t.astype(BF16),
                                (((1,), (1,)), ((), ())), preferred_element_type=F32)
            o = o + jnp.dot(scores[hh].astype(BF16), v[hh], preferred_element_type=F32)
            g_last = g[c - 1:c]
            k_dec = (k[hh] * jnp.exp2(g_last - g)).astype(BF16)
            state_sc[hh] = state_t * jnp.exp2(g_last) + lax.dot_general(
                v[hh], k_dec, (((0,), (0,)), ((), ())), preferred_element_type=F32)
            inv = lax.rsqrt(jnp.mean(o * o, axis=-1, keepdims=True) + EPS)
            o = (o * inv) * gain
            o_ref[rows, col[hh]] = (o * _silu(zg_ref[rows, col[hh]])).astype(o_ref.dtype)
        return carry

    lax.fori_loop(0, n_chunks, chunk_body, 0)


def hgrn2(proj, lb_logits, gnorm_gain, batch, seq_len, layer, heads_per_block=8, rows=512):
    m, d4 = proj.shape
    d = d4 // 4
    heads = d // HEAD_DIM
    hb = heads_per_block if heads % heads_per_block == 0 else 1
    wblk = hb * HEAD_DIM
    nh = heads // hb
    rows = _tile(seq_len, rows, HG_CHUNK)
    ns = seq_len // rows
    cum, masks, signs = _hgrn_constants()
    depth = lb_logits.shape[0]

    def zspec(section):
        return pl.BlockSpec((rows, wblk), lambda b, h, s: (b * ns + s, section * nh + h))

    return pl.pallas_call(
        functools.partial(_hgrn_kernel, layer=layer, heads_per_block=hb),
        grid=(batch, nh, ns),
        in_specs=[pl.BlockSpec(cum.shape, lambda b, h, s: (0, 0)),
                  pl.BlockSpec(masks.shape, lambda b, h, s: (0, 0, 0)),
                  pl.BlockSpec(signs.shape, lambda b, h, s: (0, 0, 0)),
                  pl.BlockSpec((depth, wblk), lambda b, h, s: (0, h)),
                  pl.BlockSpec((1, HEAD_DIM), lambda b, h, s: (0, 0)),
                  zspec(0), zspec(1), zspec(2), zspec(3)],
        out_specs=pl.BlockSpec((rows, wblk), lambda b, h, s: (b * ns + s, h)),
        out_shape=jax.ShapeDtypeStruct((m, d), BF16),
        scratch_shapes=[pltpu.VMEM((hb, HEAD_DIM, HEAD_DIM), F32)],
        compiler_params=_params("parallel", "parallel", "arbitrary"),
        name="hgrn2",
    )(jnp.asarray(cum, BF16), jnp.asarray(masks), jnp.asarray(signs), lb_logits, gnorm_gain.reshape(1, HEAD_DIM),
      proj, proj, proj, proj)


def kernel(x, mix_norm, ffn_norm, final_norm, rel_bias, attn_w_in, attn_lambda, attn_subln,
           attn_w_out, hgrn_w_in, hgrn_lb_logits, hgrn_gnorm, hgrn_w_out, ffn_w_up, ffn_conv_w,
           ffn_conv_b, ffn_w_down):
    batch, seq_len, d = x.shape
    depth = mix_norm.shape[0]
    ffn_w_down = ffn_w_down.astype(BF16)
    h = x.reshape(batch * seq_len, d)
    for layer in range(depth):
        hn = rmsnorm(h, mix_norm[layer], BF16)
        if layer % 2 == 0:
            a = layer // 2
            qkv = matmul(hn, attn_w_in, a, BF16)
            o = diff_attention(qkv, attn_lambda[a], attn_subln[a], rel_bias, batch, seq_len, layer)
            h = matmul(o, attn_w_out, a, F32, residual=h)
        else:
            r = layer // 2
            proj = matmul(hn, hgrn_w_in, r, F32)
            o = hgrn2(proj, hgrn_lb_logits, hgrn_gnorm[r], batch, seq_len, layer)
            h = matmul(o, hgrn_w_out, r, F32, residual=h)
        hn = rmsnorm(h, ffn_norm[layer], BF16)
        act = ffn_up(hn, ffn_w_up, layer, ffn_conv_w[layer], ffn_conv_b[layer], seq_len)
        h = matmul(act, ffn_w_down, layer, F32, residual=h, tm=512, tn=512)
    return rmsnorm(h, final_norm, F32).reshape(batch, seq_len, d)
```

```python
import functools
import math

import numpy as np
import jax
import jax.numpy as jnp
from jax import lax
from jax.experimental import pallas as pl
from jax.experimental.pallas import tpu as pltpu

EPS = 1e-6
HEAD_DIM = 128
REL_BUCKETS = 32
REL_MAX_DIST = 128
HG_CHUNK = 64
CONV_W = 3
NEG_BIG = -1e30
REL_BLOCK = 128
LOG2E = math.log2(math.e)

V7X_VMEM_BYTES = 64 * 1024 * 1024
VMEM_LIMIT = V7X_VMEM_BYTES - 4 * 1024 * 1024
SUBLANES = 8
LANES = 128

F32 = jnp.float32
BF16 = jnp.bfloat16


def _params(*sem):
    return pltpu.CompilerParams(dimension_semantics=sem, vmem_limit_bytes=VMEM_LIMIT)


def _tile(dim, pref, align):
    if dim <= pref:
        return dim
    t = (pref // align) * align
    while t >= align:
        if dim % t == 0:
            return t
        t -= align
    return dim


def _silu(x):
    hx = 0.5 * x
    return hx + hx * jnp.tanh(hx)


def _rmsnorm_kernel(x_ref, g_ref, o_ref):
    x = x_ref[...]
    inv = lax.rsqrt(jnp.mean(x * x, axis=-1, keepdims=True) + EPS)
    o_ref[...] = ((x * inv) * g_ref[...]).astype(o_ref.dtype)


def rmsnorm(x, gain, out_dtype, tm=512):
    m, d = x.shape
    tm = _tile(m, tm, SUBLANES)
    return pl.pallas_call(
        _rmsnorm_kernel,
        grid=(m // tm,),
        in_specs=[pl.BlockSpec((tm, d), lambda i: (i, 0)),
                  pl.BlockSpec((1, d), lambda i: (0, 0))],
        out_specs=pl.BlockSpec((tm, d), lambda i: (i, 0)),
        out_shape=jax.ShapeDtypeStruct((m, d), out_dtype),
        compiler_params=_params("parallel"),
        name="rmsnorm",
    )(x, gain.reshape(1, d))


def _matmul_kernel(x_ref, w_ref, *rest, has_res, cast_w):
    rest = list(rest)
    r_ref = rest.pop(0) if has_res else None
    o_ref = rest.pop(0)
    if cast_w:
        w_sc, = rest

        @pl.when(pl.program_id(1) == 0)
        def _():
            w_sc[...] = w_ref[...].astype(BF16)

        w = w_sc[...]
    else:
        w = w_ref[...]
    acc = jnp.dot(x_ref[...], w, preferred_element_type=F32)
    if has_res:
        acc = r_ref[...] + acc
    o_ref[...] = acc.astype(o_ref.dtype)


def matmul(x, w, layer, out_dtype, residual=None, tm=512, tn=1024):
    m, kdim = x.shape
    n = w.shape[2]
    tm = _tile(m, tm, SUBLANES)
    tn = _tile(n, tn, LANES)
    has_res = residual is not None
    cast_w = w.dtype != BF16
    in_specs = [pl.BlockSpec((tm, kdim), lambda j, i: (i, 0)),
                pl.BlockSpec((None, kdim, tn), lambda j, i: (layer, 0, j))]
    args = [x, w]
    if has_res:
        in_specs.append(pl.BlockSpec((tm, tn), lambda j, i: (i, j)))
        args.append(residual)
    return pl.pallas_call(
        functools.partial(_matmul_kernel, has_res=has_res, cast_w=cast_w),
        grid=(n // tn, m // tm),
        in_specs=in_specs,
        out_specs=pl.BlockSpec((tm, tn), lambda j, i: (i, j)),
        out_shape=jax.ShapeDtypeStruct((m, n), out_dtype),
        scratch_shapes=[pltpu.VMEM((kdim, tn), BF16)] if cast_w else [],
        compiler_params=_params("parallel", "arbitrary"),
        name="matmul_res" if has_res else "matmul",
    )(*args)


def _ffn_up_kernel(x_ref, wg_ref, wv_ref, cw_ref, cb_ref, o_ref, w_sc, tail_sc, *,
                   tiles_per_seq, n_sub):
    i = pl.program_id(1)
    tm, tn = o_ref.shape
    sub = tm // n_sub

    @pl.when(i == 0)
    def _():
        w_sc[:, :tn] = wg_ref[...].astype(BF16)
        w_sc[:, tn:] = wv_ref[...].astype(BF16)

    @pl.when((i % tiles_per_seq) == 0)
    def _():
        tail_sc[...] = jnp.zeros(tail_sc.shape, F32)

    cw = cw_ref[...]
    cb = cb_ref[...]
    row = lax.broadcasted_iota(jnp.int32, (sub, tn), 0)
    for c in range(n_sub):
        r0 = c * sub
        up = jnp.dot(x_ref[r0:r0 + sub, :], w_sc[...], preferred_element_type=F32)
        gate = up[:, :tn]
        val = up[:, tn:]
        prev1 = tail_sc[SUBLANES - 1:SUBLANES, :]
        prev2 = tail_sc[SUBLANES - 2:SUBLANES - 1, :]
        tail_sc[...] = gate[sub - SUBLANES:, :]
        g1 = jnp.where(row >= 1, pltpu.roll(gate, 1, 0), prev1)
        g2 = jnp.where(row >= 2, pltpu.roll(gate, 2, 0), jnp.where(row == 1, prev1, prev2))
        conv = cb + g2 * cw[0:1, :]
        conv = conv + g1 * cw[1:2, :]
        conv = conv + gate * cw[2:3, :]
        o_ref[r0:r0 + sub, :] = (_silu(conv) * val).astype(o_ref.dtype)


def ffn_up(x, w_up, layer, conv_w, conv_b, seq_len, tm=2048, tn=256, n_sub=4):
    m, d = x.shape
    f = conv_w.shape[1]
    tn = _tile(f, tn, LANES)
    tm = _tile(seq_len, tm, SUBLANES)
    nj = f // tn
    return pl.pallas_call(
        functools.partial(_ffn_up_kernel, tiles_per_seq=seq_len // tm, n_sub=n_sub),
        grid=(nj, m // tm),
        in_specs=[pl.BlockSpec((tm, d), lambda j, i: (i, 0)),
                  pl.BlockSpec((None, d, tn), lambda j, i: (layer, 0, j)),
                  pl.BlockSpec((None, d, tn), lambda j, i: (layer, 0, nj + j)),
                  pl.BlockSpec((CONV_W, tn), lambda j, i: (0, j)),
                  pl.BlockSpec((1, tn), lambda j, i: (0, j))],
        out_specs=pl.BlockSpec((tm, tn), lambda j, i: (i, j)),
        out_shape=jax.ShapeDtypeStruct((m, f), BF16),
        scratch_shapes=[pltpu.VMEM((d, 2 * tn), BF16),
                        pltpu.VMEM((SUBLANES, tn), F32)],
        compiler_params=_params("arbitrary", "arbitrary"),
        name="ffn_up",
    )(x, w_up, w_up, conv_w, conv_b.reshape(1, f))


def _lane_tile(x, reps):
    return x if reps == 1 else jnp.concatenate([x] * reps, axis=1)


def _lane_fold(x):
    out = x[:, :LANES]
    for b in range(1, x.shape[1] // LANES):
        out = out + x[:, b * LANES:(b + 1) * LANES]
    return out


def _attn_kernel(lam_ref, gain_ref, d0_ref, d1_ref, q_ref, k_ref, v_ref, o_ref,
                 bd_sc, q_sc, s_sc, m_sc, l_sc, acc_sc, *, lam_init, t):
    dh = HEAD_DIM
    nb = t // REL_BLOCK

    blocks = {0: d0_ref[0], 1: d1_ref[0]}
    zero = jnp.zeros((REL_BLOCK, REL_BLOCK), F32)
    neg = jnp.full((REL_BLOCK, REL_BLOCK), NEG_BIG, F32)
    for r in range(nb):
        for c in range(nb):
            blk = blocks.get(r - c, zero if r > c else neg)
            bd_sc[r * REL_BLOCK:(r + 1) * REL_BLOCK, c * REL_BLOCK:(c + 1) * REL_BLOCK] = blk

    def logits(j, slot):
        kt = k_ref[pl.ds(pl.multiple_of(j * t, t), t), :]
        for c in range(2):
            s_sc[slot, c] = lax.dot_general(q_sc[:, c * dh:(c + 1) * dh], kt[:, c * dh:(c + 1) * dh],
                                            (((1,), (1,)), ((), ())), preferred_element_type=F32)

    def softmax_pv(j, slot, bias_ref):
        vt = v_ref[pl.ds(pl.multiple_of(j * t, t), t), :]
        for c in range(2):
            s = s_sc[slot, c]
            if bias_ref is not None:
                s = s + bias_ref[...]
            m_prev = m_sc[c]
            m_next = jnp.maximum(m_prev, jnp.max(s, axis=1, keepdims=True))
            p = jnp.exp2(s - _lane_tile(m_next, t // LANES))
            alpha = jnp.exp2(m_prev - m_next)
            p = p.astype(BF16)
            l_sc[c] = alpha * l_sc[c] + _lane_fold(p.astype(F32))
            m_sc[c] = m_next
            pv = jnp.dot(p, vt, preferred_element_type=F32)
            acc_sc[c] = acc_sc[c] * _lane_tile(alpha, 2 * dh // LANES) + pv

    def q_tile(i, carry0):
        rows = pl.ds(pl.multiple_of(i * t, t), t)

        def init():
            q_sc[...] = (q_ref[rows, :].astype(F32) * (dh ** -0.5 * LOG2E)).astype(BF16)
            m_sc[...] = jnp.full(m_sc.shape, NEG_BIG, F32)
            l_sc[...] = jnp.zeros(l_sc.shape, F32)
            acc_sc[...] = jnp.zeros(acc_sc.shape, F32)

        n_far = jnp.maximum(i - 1, 0)
        odd = n_far % 2

        @pl.when(odd == 0)
        def _():
            init()
            logits(0, 0)

        @pl.when(odd == 1)
        def _():
            init()
            logits(0, 1)
            logits(1, 0)
            softmax_pv(0, 1, None)

        def pair_body(pi, carry):
            j = odd + 2 * pi
            logits(j + 1, 1)
            softmax_pv(j, 0, None)
            logits(j + 2, 0)
            softmax_pv(j + 1, 1, None)
            return carry

        n_pairs = n_far // 2

        def quad_body(qi, carry):
            pair_body(2 * qi, carry)
            pair_body(2 * qi + 1, carry)
            return carry

        lax.fori_loop(0, n_pairs // 2, quad_body, 0)

        @pl.when(n_pairs % 2 == 1)
        def _():
            pair_body(n_pairs - 1, 0)

        def finalize():
            o1 = acc_sc[0] / jnp.sum(l_sc[0], axis=1, keepdims=True)
            o2 = acc_sc[1] / jnp.sum(l_sc[1], axis=1, keepdims=True)
            lp = lam_ref[...]
            lam = (jnp.exp(jnp.sum(lp[0:1] * lp[1:2], axis=1, keepdims=True))
                   - jnp.exp(jnp.sum(lp[2:3] * lp[3:4], axis=1, keepdims=True)) + lam_init)
            a = o1 - lam * o2
            inv = lax.rsqrt(jnp.mean(a * a, axis=-1, keepdims=True) + EPS)
            o_ref[rows, :] = (((a * inv) * gain_ref[...]) * (1.0 - lam_init)).astype(o_ref.dtype)

        @pl.when(i == 0)
        def _():
            softmax_pv(0, 0, bd_sc)
            finalize()

        @pl.when(i >= 1)
        def _():
            logits(i, 1)
            for c in range(2):
                s_sc[0, c, 0:REL_BLOCK, t - REL_BLOCK:t] = (
                    s_sc[0, c, 0:REL_BLOCK, t - REL_BLOCK:t] + blocks[1])
            softmax_pv(i - 1, 0, None)
            softmax_pv(i, 1, bd_sc)
            finalize()

        return carry0

    lax.fori_loop(0, q_ref.shape[0] // t, q_tile, 0)


def _t5_causal_bucket(dist):
    n = jnp.maximum(dist, 0)
    max_exact = REL_BUCKETS // 2
    nf = jnp.maximum(n, 1).astype(F32)
    large = max_exact + (jnp.log(nf / max_exact) / math.log(REL_MAX_DIST / max_exact)
                         * (REL_BUCKETS - max_exact)).astype(jnp.int32)
    large = jnp.minimum(large, REL_BUCKETS - 1)
    return jnp.where(n < max_exact, n, large)


def _bias_blocks(rel_bias):
    rel_by_dist = rel_bias[_t5_causal_bucket(jnp.arange(2 * REL_BLOCK))]
    table = ((rel_by_dist - rel_bias[REL_BUCKETS - 1][None, :]) * LOG2E).T
    r = np.arange(REL_BLOCK)[:, None]
    c = np.arange(REL_BLOCK)[None, :]
    d0 = jnp.where(jnp.asarray(r >= c)[None], table[:, np.maximum(r - c, 0)], NEG_BIG)
    d1 = table[:, REL_BLOCK + r - c]
    return d0, d1


def diff_attention(qkv, lam_params, subln_gain, rel_bias, batch, seq_len, layer_idx, t=512):
    m, d3 = qkv.shape
    d = d3 // 3
    dv = 2 * HEAD_DIM
    heads = d // dv
    t = _tile(seq_len, t, REL_BLOCK)
    lam_init = 0.8 - 0.6 * math.exp(-0.3 * layer_idx)
    d0, d1 = _bias_blocks(rel_bias)
    blk = (1, REL_BLOCK, REL_BLOCK)
    return pl.pallas_call(
        functools.partial(_attn_kernel, lam_init=lam_init, t=t),
        grid=(batch, heads),
        in_specs=[pl.BlockSpec((4, HEAD_DIM), lambda b, h: (0, 0)),
                  pl.BlockSpec((1, dv), lambda b, h: (0, 0)),
                  pl.BlockSpec(blk, lambda b, h: (h, 0, 0)),
                  pl.BlockSpec(blk, lambda b, h: (h, 0, 0)),
                  pl.BlockSpec((seq_len, dv), lambda b, h: (b, h)),
                  pl.BlockSpec((seq_len, dv), lambda b, h: (b, heads + h)),
                  pl.BlockSpec((seq_len, dv), lambda b, h: (b, 2 * heads + h))],
        out_specs=pl.BlockSpec((seq_len, dv), lambda b, h: (b, h)),
        out_shape=jax.ShapeDtypeStruct((m, d), BF16),
        scratch_shapes=[pltpu.VMEM((t, t), F32),
                        pltpu.VMEM((t, dv), BF16),
                        pltpu.VMEM((2, 2, t, t), F32),
                        pltpu.VMEM((2, t, LANES), F32),
                        pltpu.VMEM((2, t, LANES), F32),
                        pltpu.VMEM((2, t, dv), F32)],
        compiler_params=_params("parallel", "parallel"),
        name="diff_attention",
    )(lam_params, subln_gain.reshape(1, dv), d0, d1, qkv, qkv, qkv)


def _hgrn_levels():
    return [HG_CHUNK >> l for l in range(int(math.log2(HG_CHUNK)) - 1)]


def _hgrn_constants():
    c = HG_CHUNK
    t = np.arange(c)[:, None]
    s = np.arange(c)[None, :]
    tril = (s <= t).astype(np.float32)
    masks = []
    for n in _hgrn_levels() + [2]:
        same = (t // n) == (s // n)
        masks.append(same & (t % n >= n // 2) & (s % n < n // 2))
    masks.append(t == s)
    signs = [np.where(t % n >= n // 2, 1.0, -1.0) * np.ones((1, HEAD_DIM)) for n in _hgrn_levels()]
    return tril, np.stack(masks).astype(np.float32), np.stack(signs).astype(np.float32)


def _level_reference(g, n, low_half):
    def row(r):
        return jnp.broadcast_to(g[r:r + 1, :], (SUBLANES, g.shape[1]))
    groups = []
    for gi in range(g.shape[0] // SUBLANES):
        base = gi * SUBLANES
        if n >= SUBLANES:
            groups.append(row((base // n) * n + n // 2 - 1))
        else:
            groups.append(jnp.where(low_half, row(base + 1), row(base + 5)))
    return jnp.concatenate(groups, axis=0)


def _hgrn_kernel(cum_ref, mask_ref, sign_ref, lbl_ref, gn_ref, zq_ref, zf_ref, vi_ref, zg_ref, o_ref,
                 state_sc, *, layer, heads_per_block):
    c = HG_CHUNK
    dk = HEAD_DIM
    n_lvl = len(_hgrn_levels())

    @pl.when(pl.program_id(2) == 0)
    def _():
        state_sc[...] = jnp.zeros(state_sc.shape, F32)

    lg = lbl_ref[...]
    e = jnp.exp(lg - jnp.max(lg, axis=0, keepdims=True))
    p = e / jnp.sum(e, axis=0, keepdims=True)
    lb_all = jnp.sum(p[1:layer + 1], axis=0, keepdims=True)

    cum = cum_ref[...]
    gain = gn_ref[...]
    odd_row = (lax.broadcasted_iota(jnp.int32, (c, dk), 0) % 2) == 1
    low_half = (lax.broadcasted_iota(jnp.int32, (SUBLANES, dk), 0) % SUBLANES) < SUBLANES // 2
    levels = _hgrn_levels()
    n_chunks = zq_ref.shape[0] // c

    def chunk_body(ci, carry):
        rows = pl.ds(pl.multiple_of(ci * c, c), c)
        heads = range(heads_per_block)
        col = [slice(hh * dk, (hh + 1) * dk) for hh in heads]

        q, f, k, v, cs = [], [], [], [], []
        for hh in heads:
            q.append(_silu(zq_ref[rows, col[hh]]))
            lb = lb_all[:, col[hh]]
            fh = lb + (1.0 - lb) / (1.0 + jnp.exp(-zf_ref[rows, col[hh]]))
            f.append(fh)
            k.append(1.0 - fh)
            v.append(vi_ref[rows, col[hh]].astype(BF16))
            logf = jnp.log2(fh)
            hi = logf.astype(BF16)
            r1 = logf - hi.astype(F32)
            mid = r1.astype(BF16)
            lo = (r1 - mid.astype(F32)).astype(BF16)
            x = jnp.dot(cum, jnp.concatenate([hi, mid, lo], axis=1),
                        preferred_element_type=F32)
            cs.append((x[:, :dk] + x[:, dk:2 * dk]) + x[:, 2 * dk:])

        scores = []
        for hh in heads:
            sc = None
            for l in range(n_lvl + 2):
                if l < n_lvl:
                    d = cs[hh] - _level_reference(cs[hh], levels[l], low_half)
                    w = jnp.exp2(d * sign_ref[l])
                    ql, kl = q[hh] * w, k[hh] * w
                elif l == n_lvl:
                    ql, kl = q[hh] * jnp.where(odd_row, f[hh], 1.0), k[hh]
                else:
                    ql, kl = q[hh], k[hh]
                s_l = lax.dot_general(ql.astype(BF16), kl.astype(BF16),
                                      (((1,), (1,)), ((), ())), preferred_element_type=F32)
                s_l = s_l * mask_ref[l]
                sc = s_l if sc is None else sc + s_l
            scores.append(sc)

        for hh in heads:
            g = cs[hh]
            state_t = state_sc[hh]
            o = lax.dot_general((q[hh] * jnp.exp2(g)).astype(BF16), state_t.astype(BF16),
                                (((1,), (1,)), ((), ())), preferred_element_type=F32)
            o = o + jnp.dot(scores[hh].astype(BF16), v[hh], preferred_element_type=F32)
            g_last = g[c - 1:c]
            k_dec = (k[hh] * jnp.exp2(g_last - g)).astype(BF16)
            state_sc[hh] = state_t * jnp.exp2(g_last) + lax.dot_general(
                v[hh], k_dec, (((0,), (0,)), ((), ())), preferred_element_type=F32)
            inv = lax.rsqrt(jnp.mean(o * o, axis=-1, keepdims=True) + EPS)
            o = (o * inv) * gain
            o_ref[rows, col[hh]] = (o * _silu(zg_ref[rows, col[hh]])).astype(o_ref.dtype)
        return carry

    lax.fori_loop(0, n_chunks, chunk_body, 0)


def hgrn2(proj, lb_logits, gnorm_gain, batch, seq_len, layer, heads_per_block=8, rows=512):
    m, d4 = proj.shape
    d = d4 // 4
    heads = d // HEAD_DIM
    hb = heads_per_block if heads % heads_per_block == 0 else 1
    wblk = hb * HEAD_DIM
    nh = heads // hb
    rows = _tile(seq_len, rows, HG_CHUNK)
    ns = seq_len // rows
    cum, masks, signs = _hgrn_constants()
    depth = lb_logits.shape[0]

    def zspec(section):
        return pl.BlockSpec((rows, wblk), lambda b, h, s: (b * ns + s, section * nh + h))

    return pl.pallas_call(
        functools.partial(_hgrn_kernel, layer=layer, heads_per_block=hb),
        grid=(batch, nh, ns),
        in_specs=[pl.BlockSpec(cum.shape, lambda b, h, s: (0, 0)),
                  pl.BlockSpec(masks.shape, lambda b, h, s: (0, 0, 0)),
                  pl.BlockSpec(signs.shape, lambda b, h, s: (0, 0, 0)),
                  pl.BlockSpec((depth, wblk), lambda b, h, s: (0, h)),
                  pl.BlockSpec((1, HEAD_DIM), lambda b, h, s: (0, 0)),
                  zspec(0), zspec(1), zspec(2), zspec(3)],
        out_specs=pl.BlockSpec((rows, wblk), lambda b, h, s: (b * ns + s, h)),
        out_shape=jax.ShapeDtypeStruct((m, d), BF16),
        scratch_shapes=[pltpu.VMEM((hb, HEAD_DIM, HEAD_DIM), F32)],
        compiler_params=_params("parallel", "parallel", "arbitrary"),
        name="hgrn2",
    )(jnp.asarray(cum, BF16), jnp.asarray(masks), jnp.asarray(signs), lb_logits, gnorm_gain.reshape(1, HEAD_DIM),
      proj, proj, proj, proj)


def kernel(x, mix_norm, ffn_norm, final_norm, rel_bias, attn_w_in, attn_lambda, attn_subln,
           attn_w_out, hgrn_w_in, hgrn_lb_logits, hgrn_gnorm, hgrn_w_out, ffn_w_up, ffn_conv_w,
           ffn_conv_b, ffn_w_down):
    batch, seq_len, d = x.shape
    depth = mix_norm.shape[0]
    ffn_w_down = ffn_w_down.astype(BF16)
    h = x.reshape(batch * seq_len, d)
    for layer in range(depth):
        hn = rmsnorm(h, mix_norm[layer], BF16)
        if layer % 2 == 0:
            a = layer // 2
            qkv = matmul(hn, attn_w_in, a, BF16)
            o = diff_attention(qkv, attn_lambda[a], attn_subln[a], rel_bias, batch, seq_len, layer)
            h = matmul(o, attn_w_out, a, F32, residual=h)
        else:
            r = layer // 2
            proj = matmul(hn, hgrn_w_in, r, F32)
            o = hgrn2(proj, hgrn_lb_logits, hgrn_gnorm[r], batch, seq_len, layer)
            h = matmul(o, hgrn_w_out, r, F32, residual=h)
        hn = rmsnorm(h, ffn_norm[layer], BF16)
        act = ffn_up(hn, ffn_w_up, layer, ffn_conv_w[layer], ffn_conv_b[layer], seq_len)
        h = matmul(act, ffn_w_down, layer, F32, residual=h, tm=512, tn=512)
    return rmsnorm(h, final_norm, F32).reshape(batch, seq_len, d)
```
